```python
import math
import jax, jax.numpy as jnp
from jax import lax
import numpy as np

D_MODEL = 4096
BATCH = 4
SEQ = 4096
DEPTH = 2

CONV_WIDTH = D_MODEL // 4
CONV_K = 3
POOL_WIDTH = D_MODEL // 4
POOL_WINDOWS = (2, 4, 8, 16)
POOL_GROUPS = len(POOL_WINDOWS)
POOL_GROUP_DIM = POOL_WIDTH // POOL_GROUPS
DA_HEADS = 8
DA_HEAD_DIM = 128
DA_WIDTH = 2 * DA_HEADS * DA_HEAD_DIM
Q_BLOCK = 128
SUBLN_EPS = 1e-5
N_BRANCHES = 3
IN_COLS = 3 * CONV_WIDTH + POOL_WIDTH + 3 * DA_WIDTH + N_BRANCHES * D_MODEL
D_FF_DENSE = 7 * D_MODEL // 2
N_EXPERTS = 8
TOP_K = 2
D_FF_EXPERT = 7 * D_MODEL // 8
N_DENSE = (DEPTH + 1) // 2
N_MOE = DEPTH // 2
NORM_EPS = 1e-6

kernel_name = "hybrid_conv_pool_diffattn_moe_block"


def rmsnorm(x, g, eps=NORM_EPS):
    xf = x.astype(jnp.float32)
    y = xf * lax.rsqrt(jnp.mean(xf * xf, axis=-1, keepdims=True) + eps)
    return (y * g.astype(jnp.float32)).astype(x.dtype)


def short_conv_branch(u, b_gate, c_gate, conv_w, w_out):
    z = c_gate * u
    zc = lax.conv_general_dilated(
        z, conv_w[:, None, :].astype(z.dtype), window_strides=(1,),
        padding=[(CONV_K - 1, 0)], dimension_numbers=('NWC', 'WIO', 'NWC'),
        feature_group_count=z.shape[-1])
    return (b_gate * zc) @ w_out


def pool_branch(u, pool_w, pool_scale, w_out):
    b, s, _ = u.shape
    ug = u.reshape(b, s, POOL_GROUPS, POOL_GROUP_DIM)
    ugf = ug.astype(jnp.float32)
    csum = jnp.cumsum(ugf, axis=1)
    t = jnp.arange(s)
    pooled = []
    for g, w in enumerate(POOL_WINDOWS):
        c = csum[:, :, g]
        shifted = jnp.pad(c, ((0, 0), (w, 0), (0, 0)))[:, :s]
        cnt = jnp.minimum(t + 1, w).astype(jnp.float32)[None, :, None]
        pooled.append((c - shifted) / cnt)
    mixed = (jnp.stack(pooled, axis=2) - ugf).astype(u.dtype)
    y = jnp.einsum('bsgc,gcd->bsgd', mixed, pool_w).reshape(b, s, POOL_WIDTH) * pool_scale
    return y @ w_out


def diff_attention_branch(q, k, v, lam_params, subln_g, lambda_init, w_out):
    b, s, _ = q.shape
    q = q.reshape(b, s, DA_HEADS, 2, DA_HEAD_DIM)
    k = k.reshape(b, s, DA_HEADS, 2, DA_HEAD_DIM)
    v = v.reshape(b, s, DA_HEADS, 2 * DA_HEAD_DIM)
    lp = lam_params.astype(jnp.float32)
    lam = jnp.exp(jnp.sum(lp[0] * lp[1])) - jnp.exp(jnp.sum(lp[2] * lp[3])) + lambda_init
    nb = s // Q_BLOCK
    qb = q.reshape(b, nb, Q_BLOCK, DA_HEADS, 2, DA_HEAD_DIM).transpose(1, 0, 2, 3, 4, 5)
    kpos = jnp.arange(s)
    scale = DA_HEAD_DIM ** -0.5

    def attend_block(args):
        i, qi = args
        sc = jnp.einsum('bqhcd,bkhcd->bhcqk', qi, k).astype(jnp.float32) * scale
        qpos = i * Q_BLOCK + jnp.arange(Q_BLOCK)
        mask = kpos[None, :] <= qpos[:, None]
        p = jax.nn.softmax(jnp.where(mask, sc, -jnp.inf), axis=-1)
        a = p[:, :, 0] - lam * p[:, :, 1]
        return jnp.einsum('bhqk,bkhe->bqhe', a, v)

    o = lax.map(attend_block, (jnp.arange(nb), qb))
    o = o.transpose(1, 0, 2, 3, 4).reshape(b, s, DA_HEADS, 2 * DA_HEAD_DIM).astype(jnp.float32)
    o = o * lax.rsqrt(jnp.mean(o * o, axis=-1, keepdims=True) + SUBLN_EPS)
    o = o * subln_g.astype(jnp.float32) * (1.0 - lambda_init)
    return o.reshape(b, s, DA_WIDTH).astype(w_out.dtype) @ w_out


def hybrid_mixer(h, w_in, conv_w, w_a_out, pool_w, pool_scale, w_p_out,
                 lam, subln, w_c_out, w_o, lambda_init):
    b, s, d = h.shape
    proj = h @ w_in
    base = 3 * CONV_WIDTH + POOL_WIDTH
    offs = [CONV_WIDTH, 2 * CONV_WIDTH, 3 * CONV_WIDTH, base,
            base + DA_WIDTH, base + 2 * DA_WIDTH, base + 3 * DA_WIDTH]
    a_u, a_b, a_c, p_u, q, k, v, gates = jnp.split(proj, offs, axis=-1)
    y_a = short_conv_branch(a_u, a_b, a_c, conv_w, w_a_out)
    y_p = pool_branch(p_u, pool_w, pool_scale, w_p_out)
    y_c = diff_attention_branch(q, k, v, lam, subln, lambda_init, w_c_out)
    g = jax.nn.sigmoid(gates).reshape(b, s, N_BRANCHES, d)
    merged = g[:, :, 0] * y_a + g[:, :, 1] * y_p + g[:, :, 2] * y_c
    return merged @ w_o


def swiglu(h, wg, wu, wd):
    return (jax.nn.silu(h @ wg) * (h @ wu)) @ wd


def moe_ffn(h, w_router, wg, wu, wd):
    b, s, d = h.shape
    t = h.reshape(b * s, d)
    logits = (t @ w_router).astype(jnp.float32)
    top_v, top_i = lax.top_k(logits, TOP_K)
    top_w = jax.nn.softmax(top_v, axis=-1)
    comb = jnp.sum(jax.nn.one_hot(top_i, N_EXPERTS, dtype=jnp.float32) * top_w[..., None], axis=1)
    y = jnp.zeros((b * s, d), jnp.float32)
    for e in range(N_EXPERTS):
        y = y + comb[:, e:e + 1] * swiglu(t, wg[e], wu[e], wd[e]).astype(jnp.float32)
    return y.astype(h.dtype).reshape(b, s, d)


def setup_inputs(seed: int = 0) -> dict:
    key = jax.random.key(seed)
    ks = jax.random.split(key, 24)
    f32 = jnp.float32

    def w(k, shape, fan_in):
        return jax.random.normal(k, shape, f32) * (fan_in ** -0.5)

    def gain(k, shape, s=0.02):
        return jnp.ones(shape, f32) + s * jax.random.normal(k, shape, f32)

    return {
        "x": jax.random.normal(ks[0], (BATCH, SEQ, D_MODEL), f32),
        "norm_mix": gain(ks[1], (DEPTH, D_MODEL)),
        "w_in": w(ks[2], (DEPTH, D_MODEL, IN_COLS), D_MODEL),
        "conv_w": w(ks[3], (DEPTH, CONV_K, CONV_WIDTH), CONV_K),
        "w_a_out": w(ks[4], (DEPTH, CONV_WIDTH, D_MODEL), CONV_WIDTH),
        "pool_w": w(ks[5], (DEPTH, POOL_GROUPS, POOL_GROUP_DIM, POOL_GROUP_DIM), POOL_GROUP_DIM),
        "pool_scale": gain(ks[6], (DEPTH, POOL_WIDTH), 0.1),
        "w_p_out": w(ks[7], (DEPTH, POOL_WIDTH, D_MODEL), POOL_WIDTH),
        "lam": 0.1 * jax.random.normal(ks[8], (DEPTH, 4, DA_HEAD_DIM), f32),
        "subln": gain(ks[9], (DEPTH, 2 * DA_HEAD_DIM)),
        "w_c_out": w(ks[10], (DEPTH, DA_WIDTH, D_MODEL), DA_WIDTH),
        "w_o": w(ks[11], (DEPTH, D_MODEL, D_MODEL), D_MODEL),
        "norm_ffn": gain(ks[12], (DEPTH, D_MODEL)),
        "ffn_w_gate": w(ks[13], (N_DENSE, D_MODEL, D_FF_DENSE), D_MODEL),
        "ffn_w_up": w(ks[14], (N_DENSE, D_MODEL, D_FF_DENSE), D_MODEL),
        "ffn_w_down": w(ks[15], (N_DENSE, D_FF_DENSE, D_MODEL), D_FF_DENSE),
        "w_router": w(ks[16], (N_MOE, D_MODEL, N_EXPERTS), D_MODEL),
        "moe_w_gate": w(ks[17], (N_MOE, N_EXPERTS, D_MODEL, D_FF_EXPERT), D_MODEL),
        "moe_w_up": w(ks[18], (N_MOE, N_EXPERTS, D_MODEL, D_FF_EXPERT), D_MODEL),
        "moe_w_down": w(ks[19], (N_MOE, N_EXPERTS, D_FF_EXPERT, D_MODEL), D_FF_EXPERT),
        "norm_final": gain(ks[20], (D_MODEL,)),
    }


def reference(x, norm_mix, w_in, conv_w, w_a_out, pool_w, pool_scale, w_p_out, lam, subln,
              w_c_out, w_o, norm_ffn, ffn_w_gate, ffn_w_up, ffn_w_down, w_router,
              moe_w_gate, moe_w_up, moe_w_down, norm_final):
    for l in range(DEPTH):
        lambda_init = 0.8 - 0.6 * math.exp(-0.3 * l)
        h = rmsnorm(x, norm_mix[l])
        x = x + hybrid_mixer(h, w_in[l], conv_w[l], w_a_out[l], pool_w[l], pool_scale[l],
                             w_p_out[l], lam[l], subln[l], w_c_out[l], w_o[l], lambda_init)
        h = rmsnorm(x, norm_ffn[l])
        if l % 2 == 0:
            j = l // 2
            x = x + swiglu(h, ffn_w_gate[j], ffn_w_up[j], ffn_w_down[j])
        else:
            j = l // 2
            x = x + moe_ffn(h, w_router[j], moe_w_gate[j], moe_w_up[j], moe_w_down[j])
    return rmsnorm(x, norm_final)
```

```python
import functools
import math

import jax
import jax.numpy as jnp
from jax import lax
from jax.experimental import pallas as pl
from jax.experimental.pallas import tpu as pltpu

F32 = jnp.float32
BF16 = jnp.bfloat16

NORM_EPS = 1e-6
SUBLN_EPS = 1e-5
CONV_K = 3
POOL_WINDOWS = (2, 4, 8, 16)
DA_HEADS = 8
DA_HEAD_DIM = 128
N_BRANCHES = 3
TOP_K = 2

LANES = 128
BF16_ROWS = 16
MXU_EDGE = 256
VMEM_BYTES = 64 * 1024 * 1024
VMEM_LIMIT = VMEM_BYTES - 8 * 1024 * 1024

HALO = BF16_ROWS
assert HALO >= max(POOL_WINDOWS) and HALO >= CONV_K


def _tile(dim, pref, unit=LANES):
    if dim <= pref:
        return dim
    t = (pref // unit) * unit
    while t > unit and dim % t:
        t -= unit
    assert dim % t == 0, (dim, pref, unit)
    return t


def _params(sem):
    return pltpu.CompilerParams(dimension_semantics=sem, vmem_limit_bytes=VMEM_LIMIT)


def _rmsnorm_body(x_ref, g_ref, o_ref):
    x = x_ref[...]
    ms = jnp.mean(x * x, axis=-1, keepdims=True)
    o_ref[...] = (x * lax.rsqrt(ms + NORM_EPS) * g_ref[...]).astype(o_ref.dtype)


def rmsnorm(x, g, out_dtype):
    n, d = x.shape
    tr = _tile(n, 256, 8)
    return pl.pallas_call(
        _rmsnorm_body,
        grid=(n // tr,),
        in_specs=[pl.BlockSpec((tr, d), lambda i: (i, 0)),
                  pl.BlockSpec((1, d), lambda i: (0, 0))],
        out_specs=pl.BlockSpec((tr, d), lambda i: (i, 0)),
        out_shape=jax.ShapeDtypeStruct((n, d), out_dtype),
        compiler_params=_params(("parallel",)),
        name="rmsnorm",
    )(x, g.reshape(1, d))


def _rmsnorm_router_body(x_ref, g_ref, wr_ref, h_ref, comb_ref, *, n_experts):
    x = x_ref[...]
    ms = jnp.mean(x * x, axis=-1, keepdims=True)
    h = x * lax.rsqrt(ms + NORM_EPS) * g_ref[...]
    h_ref[...] = h.astype(h_ref.dtype)
    logits = jnp.dot(h, wr_ref[...], preferred_element_type=F32, precision=lax.Precision.HIGHEST)
    lane = lax.broadcasted_iota(jnp.int32, logits.shape, 1)
    neg = jnp.float32(-jnp.inf)
    logits = jnp.where(lane < n_experts, logits, neg)
    m1 = jnp.max(logits, axis=-1, keepdims=True)
    i1 = jnp.min(jnp.where(logits == m1, lane, LANES), axis=-1, keepdims=True)
    rest = jnp.where(lane == i1, neg, logits)
    m2 = jnp.max(rest, axis=-1, keepdims=True)
    i2 = jnp.min(jnp.where(rest == m2, lane, LANES), axis=-1, keepdims=True)
    e2 = jnp.exp(m2 - m1)
    w1 = 1.0 / (1.0 + e2)
    w2 = e2 / (1.0 + e2)
    comb_ref[...] = jnp.where(lane == i1, w1, 0.0) + jnp.where(lane == i2, w2, 0.0)


def rmsnorm_router(x, g, w_router):
    n, d = x.shape
    n_experts = w_router.shape[1]
    wr = jnp.pad(w_router, ((0, 0), (0, LANES - n_experts)))
    tr = _tile(n, 256, 8)
    return pl.pallas_call(
        functools.partial(_rmsnorm_router_body, n_experts=n_experts),
        grid=(n // tr,),
        in_specs=[pl.BlockSpec((tr, d), lambda i: (i, 0)),
                  pl.BlockSpec((1, d), lambda i: (0, 0)),
                  pl.BlockSpec((d, LANES), lambda i: (0, 0))],
        out_specs=[pl.BlockSpec((tr, d), lambda i: (i, 0)),
                   pl.BlockSpec((tr, LANES), lambda i: (i, 0))],
        out_shape=[jax.ShapeDtypeStruct((n, d), BF16),
                   jax.ShapeDtypeStruct((n, LANES), F32)],
        compiler_params=_params(("parallel",)),
        name="rmsnorm_router",
    )(x, g.reshape(1, d), wr)


def _mm_body(a_ref, w_ref, *rest, nk, has_res):
    if has_res:
        res_ref, o_ref, acc_ref = rest
    else:
        o_ref, acc_ref = rest
    k = pl.program_id(2)
    part = jnp.dot(a_ref[...], w_ref[...].astype(BF16), preferred_element_type=F32)

    @pl.when(k == 0)
    def _():
        acc_ref[...] = part

    @pl.when(k > 0)
    def _():
        acc_ref[...] += part

    @pl.when(k == nk - 1)
    def _():
        r = acc_ref[...]
        if has_res:
            r = r + res_ref[...]
        o_ref[...] = r.astype(o_ref.dtype)


def matmul(a, w, layer, out_dtype, res=None, tm=1024, tn=1024, tk=512):
    m, kd = a.shape
    n = w.shape[2]
    tm, tn, tk = _tile(m, tm), _tile(n, tn), _tile(kd, tk)
    nk = kd // tk
    in_specs = [pl.BlockSpec((tm, tk), lambda i, j, k: (i, k)),
                pl.BlockSpec((None, tk, tn), lambda i, j, k: (layer, k, j))]
    args = [a, w]
    if res is not None:
        in_specs.append(pl.BlockSpec((tm, tn), lambda i, j, k: (i, j)))
        args.append(res)
    return pl.pallas_call(
        functools.partial(_mm_body, nk=nk, has_res=res is not None),
        grid=(m // tm, n // tn, nk),
        in_specs=in_specs,
        out_specs=pl.BlockSpec((tm, tn), lambda i, j, k: (i, j)),
        out_shape=jax.ShapeDtypeStruct((m, n), out_dtype),
        scratch_shapes=[pltpu.VMEM((tm, tn), F32)],
        compiler_params=_params(("parallel", "parallel", "arbitrary")),
        name="matmul_res" if res is not None else "matmul",
    )(*args)


def _glu_body(a_ref, wg_ref, wu_ref, *rest, nk, has_scale):
    if has_scale:
        s_ref, o_ref, accg_ref, accu_ref = rest
    else:
        o_ref, accg_ref, accu_ref = rest
    k = pl.program_id(2)
    a = a_ref[...]
    pg = jnp.dot(a, wg_ref[...].astype(BF16), preferred_element_type=F32)
    pu = jnp.dot(a, wu_ref[...].astype(BF16), preferred_element_type=F32)

    @pl.when(k == 0)
    def _():
        accg_ref[...] = pg
        accu_ref[...] = pu

    @pl.when(k > 0)
    def _():
        accg_ref[...] += pg
        accu_ref[...] += pu

    @pl.when(k == nk - 1)
    def _():
        g = accg_ref[...]
        r = g * jax.nn.sigmoid(g) * accu_ref[...]
        if has_scale:
            r = r * s_ref[...]
        o_ref[...] = r.astype(o_ref.dtype)


def glu_in(a, wg, wu, layer, scale=None, tm=1024, tn=1024, tk=512):
    m, kd = a.shape
    _, n_e, _, f = wg.shape
    tm, tn, tk = _tile(m, tm), _tile(f, tn, MXU_EDGE), _tile(kd, tk)
    nk, nf = kd // tk, f // tn
    w_spec = pl.BlockSpec((None, None, tk, tn), lambda i, j, k: (layer, j // nf, k, j % nf))
    in_specs = [pl.BlockSpec((tm, tk), lambda i, j, k: (i, k)), w_spec, w_spec]
    args = [a, wg, wu]
    if scale is not None:
        in_specs.append(pl.BlockSpec((None, tm, 1), lambda i, j, k: (j // nf, i, 0)))
        args.append(scale)
    return pl.pallas_call(
        functools.partial(_glu_body, nk=nk, has_scale=scale is not None),
        grid=(m // tm, n_e * nf, nk),
        in_specs=in_specs,
        out_specs=pl.BlockSpec((tm, tn), lambda i, j, k: (i, j)),
        out_shape=jax.ShapeDtypeStruct((m, n_e * f), BF16),
        scratch_shapes=[pltpu.VMEM((tm, tn), F32), pltpu.VMEM((tm, tn), F32)],
        compiler_params=_params(("parallel", "parallel", "arbitrary")),
        name="glu_in",
    )(*args)


def _merge_body(a_ref, p_ref, c_ref, wa_ref, wp_ref, wc_ref, g0_ref, g1_ref, g2_ref, o_ref):
    def branch(x_ref, w_ref, g_ref):
        y = jnp.dot(x_ref[...], w_ref[...].astype(BF16), preferred_element_type=F32)
        return jax.nn.sigmoid(g_ref[...].astype(F32)) * y

    r = branch(a_ref, wa_ref, g0_ref) + branch(p_ref, wp_ref, g1_ref) + branch(c_ref, wc_ref, g2_ref)
    o_ref[...] = r.astype(o_ref.dtype)


def merge(ya, yp, yc, wa, wp, wc, layer, proj, gate_col, tm=1024, tn=512):
    m = ya.shape[0]
    d = wa.shape[2]
    tm, tn = _tile(m, tm), _tile(d, tn)
    assert gate_col % tn == 0
    gb, nd = gate_col // tn, d // tn

    def x_spec(x):
        return pl.BlockSpec((tm, x.shape[1]), lambda i, j: (i, 0))

    def w_spec(w):
        return pl.BlockSpec((None, w.shape[1], tn), lambda i, j: (layer, 0, j))

    def g_spec(b):
        return pl.BlockSpec((tm, tn), lambda i, j: (i, gb + b * nd + j))

    return pl.pallas_call(
        _merge_body,
        grid=(m // tm, nd),
        in_specs=[x_spec(ya), x_spec(yp), x_spec(yc), w_spec(wa), w_spec(wp), w_spec(wc),
                  g_spec(0), g_spec(1), g_spec(2)],
        out_specs=pl.BlockSpec((tm, tn), lambda i, j: (i, j)),
        out_shape=jax.ShapeDtypeStruct((m, d), BF16),
        compiler_params=_params(("parallel", "parallel")),
        name="merge",
    )(ya, yp, yc, wa, wp, wc, proj, proj, proj)


def _with_halo(prev, cur, first):
    prev = jnp.where(first, 0.0, prev)
    return jnp.concatenate([prev, cur], axis=0)


def _conv_body(u_ref, b_ref, c_ref, up_ref, cp_ref, w_ref, o_ref):
    first = pl.program_id(1) == 0
    z = c_ref[...].astype(F32) * u_ref[...].astype(F32)
    zp = cp_ref[...].astype(F32) * up_ref[...].astype(F32)
    ze = _with_halo(zp, z, first)
    z1 = pltpu.roll(ze, 1, 0)[HALO:]
    z2 = pltpu.roll(ze, 2, 0)[HALO:]
    w = w_ref[...]
    zc = w[0:1] * z2 + w[1:2] * z1 + w[2:3] * z
    o_ref[...] = (b_ref[...].astype(F32) * zc).astype(o_ref.dtype)


def short_conv(proj, conv_w, ts=512, cw=256):
    bsz, s, _ = proj.shape
    width = conv_w.shape[1]
    ts, cw = _tile(s, ts, HALO), _tile(width, cw)
    nc, hb = width // cw, ts // HALO

    def cur(off):
        return pl.BlockSpec((None, ts, cw), lambda b, i, j: (b, i, off * nc + j))

    def prev(off):
        return pl.BlockSpec((None, HALO, cw), lambda b, i, j: (b, jnp.maximum(i * hb - 1, 0), off * nc + j))

    return pl.pallas_call(
        _conv_body,
        grid=(bsz, s // ts, nc),
        in_specs=[cur(0), cur(1), cur(2), prev(0), prev(2),
                  pl.BlockSpec((CONV_K, cw), lambda b, i, j: (0, j))],
        out_specs=pl.BlockSpec((None, ts, cw), lambda b, i, j: (b, i, j)),
        out_shape=jax.ShapeDtypeStruct((bsz, s, width), BF16),
        compiler_params=_params(("parallel", "parallel", "parallel")),
        name="short_conv",
    )(proj, proj, proj, proj, proj, conv_w)


def _pool_body(u_ref, up_ref, pw_ref, ps_ref, o_ref, *, ts):
    i, g = pl.program_id(1), pl.program_id(2)
    u = u_ref[...].astype(F32)
    ue = _with_halo(up_ref[...].astype(F32), u, i == 0)
    t1 = i * ts + lax.broadcasted_iota(jnp.int32, u.shape, 0) + 1
    pooled = jnp.zeros_like(u)
    s, span = ue, 1
    for gi, w in enumerate(POOL_WINDOWS):
        assert w == 2 * span
        s = s + pltpu.roll(s, span, 0)
        span = w
        mean = s[HALO:] / jnp.minimum(t1, w).astype(F32)
        pooled = jnp.where(g == gi, mean, pooled)
    mixed = (pooled - u).astype(BF16)
    y = jnp.dot(mixed, pw_ref[...].astype(BF16), preferred_element_type=F32)
    o_ref[...] = (y * ps_ref[...]).astype(o_ref.dtype)


def pool_mix(proj, col, pool_w, pool_scale, ts=512):
    bsz, s, _ = proj.shape
    n_g, gd, _ = pool_w.shape
    assert n_g == len(POOL_WINDOWS) and col % gd == 0
    ts = _tile(s, ts, HALO)
    cb, hb = col // gd, ts // HALO
    return pl.pallas_call(
        functools.partial(_pool_body, ts=ts),
        grid=(bsz, s // ts, n_g),
        in_specs=[pl.BlockSpec((None, ts, gd), lambda b, i, g: (b, i, cb + g)),
                  pl.BlockSpec((None, HALO, gd), lambda b, i, g: (b, jnp.maximum(i * hb - 1, 0), cb + g)),
                  pl.BlockSpec((None, gd, gd), lambda b, i, g: (g, 0, 0)),
                  pl.BlockSpec((1, gd), lambda b, i, g: (0, g))],
        out_specs=pl.BlockSpec((None, ts, gd), lambda b, i, g: (b, i, g)),
        out_shape=jax.ShapeDtypeStruct((bsz, s, n_g * gd), BF16),
        compiler_params=_params(("parallel", "parallel", "parallel")),
        name="pool_mix",
    )(proj, proj, pool_w, pool_scale.reshape(1, n_g * gd))


def _attn_body(lam_ref, g_ref, q_ref, k_ref, v_ref, o_ref, m_ref, l_ref, acc_ref, *, tq, lambda_init):
    dh = DA_HEAD_DIM
    i = pl.program_id(2)
    q = (q_ref[...].astype(F32) * (dh ** -0.5)).astype(BF16)
    qc = (q[:, :dh], q[:, dh:])
    m_ref[...] = jnp.full_like(m_ref, -jnp.inf)
    l_ref[...] = jnp.zeros_like(l_ref)
    acc_ref[...] = jnp.zeros_like(acc_ref)

    def step(j, masked):
        off = pl.multiple_of(j * tq, tq)
        ks = k_ref[pl.ds(off, tq), :]
        vs = v_ref[pl.ds(off, tq), :]
        for c in range(2):
            s = lax.dot_general(qc[c], ks[:, c * dh:(c + 1) * dh], (((1,), (1,)), ((), ())),
                                preferred_element_type=F32)
            if masked:
                row = lax.broadcasted_iota(jnp.int32, s.shape, 0)
                col = lax.broadcasted_iota(jnp.int32, s.shape, 1)
                s = jnp.where(col <= row, s, -jnp.inf)
            m_prev = m_ref[c]
            m_new = jnp.maximum(m_prev, jnp.max(s, axis=-1, keepdims=True))
            alpha = jnp.exp(m_prev - m_new)
            p = jnp.exp(s - m_new)
            l_ref[c] = alpha * l_ref[c] + jnp.sum(p, axis=-1, keepdims=True)
            acc_ref[c] = alpha * acc_ref[c] + jnp.dot(p.astype(BF16), vs, preferred_element_type=F32)
            m_ref[c] = m_new

    def body(j, carry):
        step(j, False)
        return carry

    lax.fori_loop(0, i, body, 0)
    step(i, True)

    lp = lam_ref[...]
    lam = (jnp.exp(jnp.sum(lp[0:1] * lp[1:2], axis=-1, keepdims=True))
           - jnp.exp(jnp.sum(lp[2:3] * lp[3:4], axis=-1, keepdims=True)) + lambda_init)
    o = acc_ref[0] / l_ref[0] - lam * (acc_ref[1] / l_ref[1])
    o = o * lax.rsqrt(jnp.mean(o * o, axis=-1, keepdims=True) + SUBLN_EPS)
    o_ref[...] = (o * g_ref[...] * (1.0 - lambda_init)).astype(o_ref.dtype)


def diff_attention(proj, col, lam, subln_g, lambda_init, tq=256):
    bsz, s, _ = proj.shape
    hw = 2 * DA_HEAD_DIM
    assert col % hw == 0
    tq = _tile(s, tq)
    qb = col // hw
    kb, vb = qb + DA_HEADS, qb + 2 * DA_HEADS
    return pl.pallas_call(
        functools.partial(_attn_body, tq=tq, lambda_init=lambda_init),
        grid=(bsz, DA_HEADS, s // tq),
        in_specs=[pl.BlockSpec((4, DA_HEAD_DIM), lambda b, h, i: (0, 0)),
                  pl.BlockSpec((1, hw), lambda b, h, i: (0, 0)),
                  pl.BlockSpec((None, tq, hw), lambda b, h, i: (b, i, qb + h)),
                  pl.BlockSpec((None, s, hw), lambda b, h, i: (b, 0, kb + h)),
                  pl.BlockSpec((None, s, hw), lambda b, h, i: (b, 0, vb + h))],
        out_specs=pl.BlockSpec((None, tq, hw), lambda b, h, i: (b, i, h)),
        out_shape=jax.ShapeDtypeStruct((bsz, s, DA_HEADS * hw), BF16),
        scratch_shapes=[pltpu.VMEM((2, tq, 1), F32), pltpu.VMEM((2, tq, 1), F32),
                        pltpu.VMEM((2, tq, hw), F32)],
        compiler_params=_params(("parallel", "parallel", "parallel")),
        name="diff_attention",
    )(lam, subln_g.reshape(1, hw), proj, proj, proj)


def _mixer(x, h, l, w_in, conv_w, w_a_out, pool_w, pool_scale, w_p_out, lam, subln, w_c_out, w_o,
           lambda_init, bsz, s):
    n, d = x.shape
    cw, pw, aw = conv_w.shape[2], pool_scale.shape[1], w_c_out.shape[1]
    proj = matmul(h, w_in, l, BF16)
    proj3 = proj.reshape(bsz, s, proj.shape[1])
    ya = short_conv(proj3, conv_w[l]).reshape(n, cw)
    yp = pool_mix(proj3, 3 * cw, pool_w[l], pool_scale[l]).reshape(n, pw)
    yc = diff_attention(proj3, 3 * cw + pw, lam[l], subln[l], lambda_init).reshape(n, aw)
    merged = merge(ya, yp, yc, w_a_out, w_p_out, w_c_out, l, proj, 3 * cw + pw + 3 * aw)
    return matmul(merged, w_o, l, F32, res=x)


def kernel(x, norm_mix, w_in, conv_w, w_a_out, pool_w, pool_scale, w_p_out, lam, subln, w_c_out, w_o,
           norm_ffn, ffn_w_gate, ffn_w_up, ffn_w_down, w_router, moe_w_gate, moe_w_up, moe_w_down,
           norm_final):
    bsz, s, d = x.shape
    depth = norm_mix.shape[0]
    n = bsz * s
    x = x.reshape(n, d)
    for l in range(depth):
        lambda_init = 0.8 - 0.6 * math.exp(-0.3 * l)
        h = rmsnorm(x, norm_mix[l], BF16)
        x = _mixer(x, h, l, w_in, conv_w, w_a_out, pool_w, pool_scale, w_p_out, lam, subln, w_c_out, w_o,
                   lambda_init, bsz, s)
        j = l // 2
        if l % 2 == 0:
            h = rmsnorm(x, norm_ffn[l], BF16)
            act = glu_in(h, ffn_w_gate[:, None], ffn_w_up[:, None], j)
            x = matmul(act, ffn_w_down, j, F32, res=x)
        else:
            h, comb = rmsnorm_router(x, norm_ffn[l], w_router[j])
            n_moe, n_e, f, _ = moe_w_down.shape
            scale = comb[:, :n_e].T.reshape(n_e, n, 1)
            act = glu_in(h, moe_w_gate, moe_w_up, j, scale=scale)
            x = matmul(act, moe_w_down.reshape(n_moe, n_e * f, d), j, F32, res=x)
    return rmsnorm(x, norm_final, F32).reshape(bsz, s, d)
```

```python
import functools
import math

import jax
import jax.numpy as jnp
from jax import lax
from jax.experimental import pallas as pl
from jax.experimental.pallas import tpu as pltpu

F32 = jnp.float32
BF16 = jnp.bfloat16
U32 = jnp.uint32

NORM_EPS = 1e-6
SUBLN_EPS = 1e-5
CONV_K = 3
POOL_WINDOWS = (2, 4, 8, 16)
DA_HEADS = 8
DA_HEAD_DIM = 128
N_BRANCHES = 3
TOP_K = 2

LANES = 128
BF16_ROWS = 16
MXU_EDGE = 256
VMEM_BYTES = 64 * 1024 * 1024
VMEM_LIMIT = VMEM_BYTES - 8 * 1024 * 1024

HALO = BF16_ROWS
assert HALO >= max(POOL_WINDOWS) and HALO >= CONV_K
LOG2E = math.log2(math.e)
HI16 = 0xFFFF0000


def _tile(dim, pref, unit=LANES):
    if dim <= pref:
        return dim
    t = (pref // unit) * unit
    while t > unit and dim % t:
        t -= unit
    assert dim % t == 0, (dim, pref, unit)
    return t


def _params(sem):
    return pltpu.CompilerParams(dimension_semantics=sem, vmem_limit_bytes=VMEM_LIMIT)


def _resident(block_shape, index_map):
    return pl.BlockSpec(block_shape, index_map, pipeline_mode=pl.Buffered(1))


def _rms(x, g):
    ms = jnp.mean(x * x, axis=-1, keepdims=True)
    return x * lax.rsqrt(ms + NORM_EPS) * g


def _rmsnorm_body(x_ref, g_ref, o_ref):
    o_ref[...] = _rms(x_ref[...], g_ref[...]).astype(o_ref.dtype)


def rmsnorm(x, g, out_dtype):
    n, d = x.shape
    tr = _tile(n, 256, 8)
    return pl.pallas_call(
        _rmsnorm_body,
        grid=(n // tr,),
        in_specs=[pl.BlockSpec((tr, d), lambda i: (i, 0)),
                  pl.BlockSpec((1, d), lambda i: (0, 0))],
        out_specs=pl.BlockSpec((tr, d), lambda i: (i, 0)),
        out_shape=jax.ShapeDtypeStruct((n, d), out_dtype),
        compiler_params=_params(("parallel",)),
        name="rmsnorm",
    )(x, g.reshape(1, d))


def _pack_halves(h):
    half = h.shape[1] // 2
    lo = lax.bitcast_convert_type(h[:, :half].astype(BF16).astype(F32), U32)
    hi = lax.bitcast_convert_type(h[:, half:].astype(BF16).astype(F32), U32)
    return (lo >> 16) | (hi & jnp.uint32(HI16))


def _unpack_halves(p):
    lo = lax.bitcast_convert_type(p << 16, F32).astype(BF16)
    hi = lax.bitcast_convert_type(p & jnp.uint32(HI16), F32).astype(BF16)
    return lo, hi


def _rmsnorm_router_body(x_ref, g_ref, wr_ref, hp_ref, ei_ref, ew_ref, *, n_experts):
    h = _rms(x_ref[...], g_ref[...])
    hp_ref[...] = _pack_halves(h)
    logits = jnp.dot(h, wr_ref[...], preferred_element_type=F32, precision=lax.Precision.HIGHEST)
    lane = lax.broadcasted_iota(jnp.int32, logits.shape, 1)
    neg = jnp.float32(-jnp.inf)
    logits = jnp.where(lane < n_experts, logits, neg)
    m1 = jnp.max(logits, axis=-1, keepdims=True)
    i1 = jnp.min(jnp.where(logits == m1, lane, LANES), axis=-1, keepdims=True)
    rest = jnp.where(lane == i1, neg, logits)
    m2 = jnp.max(rest, axis=-1, keepdims=True)
    i2 = jnp.min(jnp.where(rest == m2, lane, LANES), axis=-1, keepdims=True)
    e2 = jnp.exp(m2 - m1)
    w1 = 1.0 / (1.0 + e2)
    w2 = e2 / (1.0 + e2)
    ei_ref[...] = jnp.where(lane == 0, i1, jnp.where(lane == 1, i2, 0))
    ew_ref[...] = jnp.where(lane == 0, w1, jnp.where(lane == 1, w2, 0.0))


def rmsnorm_router(x, g, w_router):
    n, d = x.shape
    n_experts = w_router.shape[1]
    wr = jnp.pad(w_router, ((0, 0), (0, LANES - n_experts)))
    tr = _tile(n, 256, 8)
    return pl.pallas_call(
        functools.partial(_rmsnorm_router_body, n_experts=n_experts),
        grid=(n // tr,),
        in_specs=[pl.BlockSpec((tr, d), lambda i: (i, 0)),
                  pl.BlockSpec((1, d), lambda i: (0, 0)),
                  pl.BlockSpec((d, LANES), lambda i: (0, 0))],
        out_specs=[pl.BlockSpec((tr, d // 2), lambda i: (i, 0)),
                   pl.BlockSpec((tr, LANES), lambda i: (i, 0)),
                   pl.BlockSpec((tr, LANES), lambda i: (i, 0))],
        out_shape=[jax.ShapeDtypeStruct((n, d // 2), U32),
                   jax.ShapeDtypeStruct((n, LANES), jnp.int32),
                   jax.ShapeDtypeStruct((n, LANES), F32)],
        compiler_params=_params(("parallel",)),
        name="rmsnorm_router",
    )(x, g.reshape(1, d), wr)


def _mm_body(a_ref, w_ref, *rest, has_res):
    r = jnp.dot(a_ref[...], w_ref[...].astype(BF16), preferred_element_type=F32)
    if has_res:
        r = r + rest[0][...]
    rest[-1][...] = r.astype(rest[-1].dtype)


def matmul(a, w, layer, out_dtype, res=None, tm=2048, tn=512):
    m, kd = a.shape
    n = w.shape[2]
    tm, tn = _tile(m, tm), _tile(n, tn)
    in_specs = [_resident((tm, kd), lambda i, j: (i, 0)),
                pl.BlockSpec((None, kd, tn), lambda i, j: (layer, 0, j))]
    args = [a, w]
    if res is not None:
        in_specs.append(pl.BlockSpec((tm, tn), lambda i, j: (i, j)))
        args.append(res)
    return pl.pallas_call(
        functools.partial(_mm_body, has_res=res is not None),
        grid=(m // tm, n // tn),
        in_specs=in_specs,
        out_specs=pl.BlockSpec((tm, tn), lambda i, j: (i, j)),
        out_shape=jax.ShapeDtypeStruct((m, n), out_dtype),
        compiler_params=_params(("parallel", "arbitrary")),
        name="matmul_res" if res is not None else "matmul",
    )(*args)


def _mm_acc_body(a_ref, w_ref, res_ref, o_ref):
    @pl.when(pl.program_id(2) == 0)
    def _():
        o_ref[...] = res_ref[...]

    o_ref[...] += jnp.dot(a_ref[...], w_ref[...].astype(BF16), preferred_element_type=F32)


def matmul_acc(a, w, layer, res, tm=2048, tn=512, tk=1792):
    m, kd = a.shape
    n = w.shape[2]
    tm, tn, tk = _tile(m, tm), _tile(n, tn), _tile(kd, tk, MXU_EDGE)
    return pl.pallas_call(
        _mm_acc_body,
        grid=(m // tm, n // tn, kd // tk),
        in_specs=[pl.BlockSpec((tm, tk), lambda i, j, k: (i, k)),
                  pl.BlockSpec((None, tk, tn), lambda i, j, k: (layer, k, j)),
                  pl.BlockSpec((tm, tn), lambda i, j, k: (i, j))],
        out_specs=pl.BlockSpec((tm, tn), lambda i, j, k: (i, j)),
        out_shape=jax.ShapeDtypeStruct((m, n), F32),
        compiler_params=_params(("parallel", "parallel", "arbitrary")),
        name="matmul_acc",
    )(a, w, res)


def _silu_mul(g, u):
    return g * jax.nn.sigmoid(g) * u


def _glu_body(a_ref, wg_ref, wu_ref, o_ref):
    a = a_ref[...]
    g = jnp.dot(a, wg_ref[...].astype(BF16), preferred_element_type=F32)
    u = jnp.dot(a, wu_ref[...].astype(BF16), preferred_element_type=F32)
    o_ref[...] = _silu_mul(g, u).astype(o_ref.dtype)


def glu_in(a, wg, wu, layer, tm=2048, tn=256):
    m, kd = a.shape
    f = wg.shape[2]
    tm, tn = _tile(m, tm), _tile(f, tn, MXU_EDGE)
    w_spec = pl.BlockSpec((None, kd, tn), lambda i, j: (layer, 0, j))
    return pl.pallas_call(
        _glu_body,
        grid=(m // tm, f // tn),
        in_specs=[_resident((tm, kd), lambda i, j: (i, 0)), w_spec, w_spec],
        out_specs=pl.BlockSpec((tm, tn), lambda i, j: (i, j)),
        out_shape=jax.ShapeDtypeStruct((m, f), BF16),
        compiler_params=_params(("parallel", "arbitrary")),
        name="glu_in",
    )(a, wg, wu)


def _merge_body(a_ref, p_ref, c_ref, wa_ref, wp_ref, wc_ref, g0_ref, g1_ref, g2_ref, o_ref):
    def branch(x_ref, w_ref, g_ref):
        y = jnp.dot(x_ref[...], w_ref[...].astype(BF16), preferred_element_type=F32)
        return jax.nn.sigmoid(g_ref[...].astype(F32)) * y

    r = branch(a_ref, wa_ref, g0_ref) + branch(p_ref, wp_ref, g1_ref) + branch(c_ref, wc_ref, g2_ref)
    o_ref[...] = r.astype(o_ref.dtype)


def merge(ya, yp, yc, wa, wp, wc, layer, proj, gate_col, tm=1024, tn=512):
    m = ya.shape[0]
    d = wa.shape[2]
    tm, tn = _tile(m, tm), _tile(d, tn)
    assert gate_col % tn == 0
    gb, nd = gate_col // tn, d // tn

    def x_spec(x):
        return pl.BlockSpec((tm, x.shape[1]), lambda i, j: (i, 0))

    def w_spec(w):
        return pl.BlockSpec((None, w.shape[1], tn), lambda i, j: (layer, 0, j))

    def g_spec(b):
        return pl.BlockSpec((tm, tn), lambda i, j: (i, gb + b * nd + j))

    return pl.pallas_call(
        _merge_body,
        grid=(m // tm, nd),
        in_specs=[x_spec(ya), x_spec(yp), x_spec(yc), w_spec(wa), w_spec(wp), w_spec(wc),
                  g_spec(0), g_spec(1), g_spec(2)],
        out_specs=pl.BlockSpec((tm, tn), lambda i, j: (i, j)),
        out_shape=jax.ShapeDtypeStruct((m, d), BF16),
        compiler_params=_params(("parallel", "parallel")),
        name="merge",
    )(ya, yp, yc, wa, wp, wc, proj, proj, proj)


def _with_halo(prev, cur, first):
    prev = jnp.where(first, 0.0, prev)
    return jnp.concatenate([prev, cur], axis=0)


def _conv_body(u_ref, b_ref, c_ref, up_ref, cp_ref, w_ref, o_ref):
    first = pl.program_id(1) == 0
    z = c_ref[...].astype(F32) * u_ref[...].astype(F32)
    zp = cp_ref[...].astype(F32) * up_ref[...].astype(F32)
    ze = _with_halo(zp, z, first)
    z1 = pltpu.roll(ze, 1, 0)[HALO:]
    z2 = pltpu.roll(ze, 2, 0)[HALO:]
    w = w_ref[...]
    zc = w[0:1] * z2 + w[1:2] * z1 + w[2:3] * z
    o_ref[...] = (b_ref[...].astype(F32) * zc).astype(o_ref.dtype)


def short_conv(proj, conv_w, ts=512, cw=256):
    bsz, s, _ = proj.shape
    width = conv_w.shape[1]
    ts, cw = _tile(s, ts, HALO), _tile(width, cw)
    nc, hb = width // cw, ts // HALO

    def cur(off):
        return pl.BlockSpec((None, ts, cw), lambda b, i, j: (b, i, off * nc + j))

    def prev(off):
        return pl.BlockSpec((None, HALO, cw), lambda b, i, j: (b, jnp.maximum(i * hb - 1, 0), off * nc + j))

    return pl.pallas_call(
        _conv_body,
        grid=(bsz, s // ts, nc),
        in_specs=[cur(0), cur(1), cur(2), prev(0), prev(2),
                  pl.BlockSpec((CONV_K, cw), lambda b, i, j: (0, j))],
        out_specs=pl.BlockSpec((None, ts, cw), lambda b, i, j: (b, i, j)),
        out_shape=jax.ShapeDtypeStruct((bsz, s, width), BF16),
        compiler_params=_params(("parallel", "parallel", "parallel")),
        name="short_conv",
    )(proj, proj, proj, proj, proj, conv_w)


def _pool_body(u_ref, up_ref, pw_ref, ps_ref, o_ref, *, ts):
    i, g = pl.program_id(1), pl.program_id(2)
    u = u_ref[...].astype(F32)
    ue = _with_halo(up_ref[...].astype(F32), u, i == 0)
    t1 = i * ts + lax.broadcasted_iota(jnp.int32, u.shape, 0) + 1
    pooled = jnp.zeros_like(u)
    s, span = ue, 1
    for gi, w in enumerate(POOL_WINDOWS):
        assert w == 2 * span
        s = s + pltpu.roll(s, span, 0)
        span = w
        mean = s[HALO:] / jnp.minimum(t1, w).astype(F32)
        pooled = jnp.where(g == gi, mean, pooled)
    mixed = (pooled - u).astype(BF16)
    y = jnp.dot(mixed, pw_ref[...].astype(BF16), preferred_element_type=F32)
    o_ref[...] = (y * ps_ref[...]).astype(o_ref.dtype)


def pool_mix(proj, col, pool_w, pool_scale, ts=512):
    bsz, s, _ = proj.shape
    n_g, gd, _ = pool_w.shape
    assert n_g == len(POOL_WINDOWS) and col % gd == 0
    ts = _tile(s, ts, HALO)
    cb, hb = col // gd, ts // HALO
    return pl.pallas_call(
        functools.partial(_pool_body, ts=ts),
        grid=(bsz, s // ts, n_g),
        in_specs=[pl.BlockSpec((None, ts, gd), lambda b, i, g: (b, i, cb + g)),
                  pl.BlockSpec((None, HALO, gd), lambda b, i, g: (b, jnp.maximum(i * hb - 1, 0), cb + g)),
                  pl.BlockSpec((None, gd, gd), lambda b, i, g: (g, 0, 0)),
                  pl.BlockSpec((1, gd), lambda b, i, g: (0, g))],
        out_specs=pl.BlockSpec((None, ts, gd), lambda b, i, g: (b, i, g)),
        out_shape=jax.ShapeDtypeStruct((bsz, s, n_g * gd), BF16),
        compiler_params=_params(("parallel", "parallel", "parallel")),
        name="pool_mix",
    )(proj, proj, pool_w, pool_scale.reshape(1, n_g * gd))


def _attn_body(lam_ref, g_ref, q_ref, k_ref, v_ref, o_ref, s_ref, m_ref, l_ref, acc_ref, *, tq, lambda_init):
    dh = DA_HEAD_DIM
    i = pl.program_id(2)
    slabs = [slice(b * LANES, (b + 1) * LANES) for b in range(tq // LANES)]
    q = (q_ref[...].astype(F32) * (dh ** -0.5 * LOG2E)).astype(BF16)
    outs = []
    for c in range(2):
        cols = slice(c * dh, (c + 1) * dh)
        qc = q[:, cols]

        def scores(j):
            kc = k_ref[pl.ds(pl.multiple_of(j * tq, tq), tq), cols]
            return lax.dot_general(qc, kc, (((1,), (1,)), ((), ())), preferred_element_type=F32)

        def keep(j, s):
            s_ref[j] = s
            m = m_ref[...]
            for sl in slabs:
                m = jnp.maximum(m, s[:, sl])
            m_ref[...] = m

        def pass1(j, carry):
            keep(j, scores(j))
            return carry

        m_ref[...] = jnp.full_like(m_ref, -jnp.inf)
        lax.fori_loop(0, i, pass1, 0)
        s = scores(i)
        row = lax.broadcasted_iota(jnp.int32, s.shape, 0)
        col = lax.broadcasted_iota(jnp.int32, s.shape, 1)
        keep(i, jnp.where(col <= row, s, -jnp.inf))
        mb = jnp.broadcast_to(jnp.max(m_ref[...], axis=-1, keepdims=True), (tq, LANES))

        def pass2(j, carry):
            s = s_ref[j]
            l = l_ref[...]
            ps = []
            for sl in slabs:
                p = jnp.exp2(s[:, sl] - mb)
                l = l + p
                ps.append(p.astype(BF16))
            l_ref[...] = l
            vc = v_ref[pl.ds(pl.multiple_of(j * tq, tq), tq), :]
            acc_ref[...] += jnp.dot(jnp.concatenate(ps, axis=1), vc, preferred_element_type=F32)
            return carry

        l_ref[...] = jnp.zeros_like(l_ref)
        acc_ref[...] = jnp.zeros_like(acc_ref)
        lax.fori_loop(0, i + 1, pass2, 0)
        outs.append(acc_ref[...] / jnp.sum(l_ref[...], axis=-1, keepdims=True))

    lp = lam_ref[...]
    lam = (jnp.exp(jnp.sum(lp[0:1] * lp[1:2], axis=-1, keepdims=True))
           - jnp.exp(jnp.sum(lp[2:3] * lp[3:4], axis=-1, keepdims=True)) + lambda_init)
    o = outs[0] - lam * outs[1]
    o = o * lax.rsqrt(jnp.mean(o * o, axis=-1, keepdims=True) + SUBLN_EPS)
    o_ref[...] = (o * g_ref[...] * (1.0 - lambda_init)).astype(o_ref.dtype)


def diff_attention(proj, col, lam, subln_g, lambda_init, tq=512):
    bsz, s, _ = proj.shape
    hw = 2 * DA_HEAD_DIM
    assert col % hw == 0
    tq = _tile(s, tq)
    qb = col // hw
    kb, vb = qb + DA_HEADS, qb + 2 * DA_HEADS
    return pl.pallas_call(
        functools.partial(_attn_body, tq=tq, lambda_init=lambda_init),
        grid=(bsz, DA_HEADS, s // tq),
        in_specs=[pl.BlockSpec((4, DA_HEAD_DIM), lambda b, h, i: (0, 0)),
                  pl.BlockSpec((1, hw), lambda b, h, i: (0, 0)),
                  pl.BlockSpec((None, tq, hw), lambda b, h, i: (b, i, qb + h)),
                  pl.BlockSpec((None, s, hw), lambda b, h, i: (b, 0, kb + h)),
                  pl.BlockSpec((None, s, hw), lambda b, h, i: (b, 0, vb + h))],
        out_specs=pl.BlockSpec((None, tq, hw), lambda b, h, i: (b, i, h)),
        out_shape=jax.ShapeDtypeStruct((bsz, s, DA_HEADS * hw), BF16),
        scratch_shapes=[pltpu.VMEM((s // tq, tq, tq), F32), pltpu.VMEM((tq, LANES), F32),
                        pltpu.VMEM((tq, LANES), F32), pltpu.VMEM((tq, hw), F32)],
        compiler_params=_params(("parallel", "parallel", "arbitrary")),
        name="diff_attention",
    )(lam, subln_g.reshape(1, hw), proj, proj, proj)


MOE_ROWS = 512
MOE_TOKENS = 256


def _route(ei, n_e):
    n = ei.shape[0]
    e = ei[:, :TOP_K]
    onehot = (e[:, :, None] == jnp.arange(n_e, dtype=jnp.int32)).astype(jnp.int32).reshape(n * TOP_K, n_e)
    incl = jnp.cumsum(onehot, axis=0)
    rank = jnp.sum((incl - onehot) * onehot, axis=1)
    tiles = (incl[-1] + MOE_ROWS - 1) // MOE_ROWS
    tile_end = jnp.cumsum(tiles)
    start = (tile_end - tiles) * MOE_ROWS
    dest = (jnp.sum(onehot * start[None, :], axis=1) + rank).reshape(n, TOP_K)
    max_tiles = (n * TOP_K) // MOE_ROWS + n_e
    n_used = tile_end[-1:]
    t = jnp.minimum(jnp.arange(max_tiles, dtype=jnp.int32), n_used[0] - 1)
    tile_expert = jnp.sum((t[:, None] >= tile_end[None, :]).astype(jnp.int32), axis=1)
    return dest.astype(jnp.int32), tile_expert.astype(jnp.int32), n_used.astype(jnp.int32), max_tiles


def _dispatch_body(dest_ref, hp_ref, init_ref, hs_ref, sem, *, tt):
    del init_ref

    def row_copy(t, k):
        return pltpu.make_async_copy(hp_ref.at[pl.ds(t, 1)], hs_ref.at[pl.ds(dest_ref[0, TOP_K * t + k], 1)], sem)

    def start(t, carry):
        for k in range(TOP_K):
            row_copy(t, k).start()
        return carry

    def wait(t, carry):
        for k in range(TOP_K):
            row_copy(t, k).wait()
        return carry

    lax.fori_loop(0, tt, start, 0)
    lax.fori_loop(0, tt, wait, 0)


def moe_dispatch(hp, dest, n_rows):
    n, w = hp.shape
    tt = _tile(n, MOE_TOKENS, 8)
    return pl.pallas_call(
        functools.partial(_dispatch_body, tt=tt),
        grid=(n // tt,),
        in_specs=[pl.BlockSpec((None, 1, TOP_K * tt), lambda i: (i, 0, 0), memory_space=pltpu.SMEM),
                  pl.BlockSpec((tt, w), lambda i: (i, 0)),
                  pl.BlockSpec(memory_space=pl.ANY)],
        out_specs=pl.BlockSpec(memory_space=pl.ANY),
        out_shape=jax.ShapeDtypeStruct((n_rows, w), U32),
        scratch_shapes=[pltpu.SemaphoreType.DMA(())],
        input_output_aliases={2: 0},
        compiler_params=_params(("arbitrary",)),
        name="moe_dispatch",
    )(dest.reshape(n // tt, 1, TOP_K * tt), hp, jnp.zeros((n_rows, w), U32))


def _zero_unused_tile(nu_ref, o_ref):
    @pl.when(pl.program_id(1) >= nu_ref[0])
    def _():
        o_ref[...] = jnp.zeros_like(o_ref)


def _moe_glu_body(te_ref, nu_ref, a_ref, wg_ref, wu_ref, o_ref):
    del te_ref
    _zero_unused_tile(nu_ref, o_ref)

    @pl.when(pl.program_id(1) < nu_ref[0])
    def _():
        lo, hi = _unpack_halves(a_ref[...])
        half = lo.shape[1]

        def proj(w_ref):
            return (jnp.dot(lo, w_ref[:half].astype(BF16), preferred_element_type=F32)
                    + jnp.dot(hi, w_ref[half:].astype(BF16), preferred_element_type=F32))

        o_ref[...] = _silu_mul(proj(wg_ref), proj(wu_ref)).astype(o_ref.dtype)


def moe_glu(hs, wg, wu, layer, tile_expert, n_used, tn=256):
    rows, half = hs.shape
    f = wg.shape[3]
    tn = _tile(f, tn, MXU_EDGE)
    n_tiles = rows // MOE_ROWS

    def row_tile(r, nu):
        return jnp.minimum(r, nu[0] - 1)

    w_spec = pl.BlockSpec((None, None, 2 * half, tn), lambda j, r, te, nu: (layer, te[r], 0, j))
    return pl.pallas_call(
        _moe_glu_body,
        grid_spec=pltpu.PrefetchScalarGridSpec(
            num_scalar_prefetch=2,
            grid=(f // tn, n_tiles),
            in_specs=[pl.BlockSpec((MOE_ROWS, half), lambda j, r, te, nu: (row_tile(r, nu), 0)), w_spec, w_spec],
            out_specs=pl.BlockSpec((MOE_ROWS, tn), lambda j, r, te, nu: (r, j))),
        out_shape=jax.ShapeDtypeStruct((rows, f), BF16),
        compiler_params=_params(("arbitrary", "arbitrary")),
        name="moe_glu",
    )(tile_expert, n_used, hs, wg, wu)


def _moe_down_body(te_ref, nu_ref, a_ref, w_ref, o_ref):
    del te_ref
    _zero_unused_tile(nu_ref, o_ref)

    @pl.when(pl.program_id(1) < nu_ref[0])
    def _():
        o_ref[...] = jnp.dot(a_ref[...], w_ref[...].astype(BF16), preferred_element_type=F32)


def moe_down(act, wd, layer, tile_expert, n_used, tn=1024):
    rows, f = act.shape
    d = wd.shape[3]
    tn = _tile(d, tn)
    n_tiles = rows // MOE_ROWS

    def row_tile(r, nu):
        return jnp.minimum(r, nu[0] - 1)

    return pl.pallas_call(
        _moe_down_body,
        grid_spec=pltpu.PrefetchScalarGridSpec(
            num_scalar_prefetch=2,
            grid=(d // tn, n_tiles),
            in_specs=[pl.BlockSpec((MOE_ROWS, f), lambda j, r, te, nu: (row_tile(r, nu), 0)),
                      pl.BlockSpec((None, None, f, tn), lambda j, r, te, nu: (layer, te[r], 0, j))],
            out_specs=pl.BlockSpec((MOE_ROWS, tn), lambda j, r, te, nu: (r, j))),
        out_shape=jax.ShapeDtypeStruct((rows, d), F32),
        compiler_params=_params(("arbitrary", "arbitrary")),
        name="moe_down",
    )(tile_expert, n_used, act, wd)


def _combine_body(dest_ref, ew_ref, x_ref, ys_ref, o_ref, ybuf, sem, *, tt):
    def row_copy(t, k):
        return pltpu.make_async_copy(ys_ref.at[pl.ds(dest_ref[0, TOP_K * t + k], 1)], ybuf.at[k, pl.ds(t, 1)], sem)

    def start(t, carry):
        for k in range(TOP_K):
            row_copy(t, k).start()
        return carry

    def wait(t, carry):
        for k in range(TOP_K):
            row_copy(t, k).wait()
        return carry

    lax.fori_loop(0, tt, start, 0)
    lax.fori_loop(0, tt, wait, 0)
    ew = ew_ref[...]
    o_ref[...] = x_ref[...] + ew[:, 0:1] * ybuf[0] + ew[:, 1:2] * ybuf[1]


def moe_combine(x, ys, dest, ew):
    n, d = x.shape
    tt = _tile(n, MOE_TOKENS, 8)
    return pl.pallas_call(
        functools.partial(_combine_body, tt=tt),
        grid=(n // tt,),
        in_specs=[pl.BlockSpec((None, 1, TOP_K * tt), lambda i: (i, 0, 0), memory_space=pltpu.SMEM),
                  pl.BlockSpec((tt, LANES), lambda i: (i, 0)),
                  pl.BlockSpec((tt, d), lambda i: (i, 0)),
                  pl.BlockSpec(memory_space=pl.ANY)],
        out_specs=pl.BlockSpec((tt, d), lambda i: (i, 0)),
        out_shape=jax.ShapeDtypeStruct((n, d), F32),
        scratch_shapes=[pltpu.VMEM((TOP_K, tt, d), F32), pltpu.SemaphoreType.DMA(())],
        compiler_params=_params(("arbitrary",)),
        name="moe_combine",
    )(dest.reshape(n // tt, 1, TOP_K * tt), ew, x, ys)


def moe_ffn(x, g, w_router, wg, wu, wd, layer):
    n_e = wg.shape[1]
    hp, ei, ew = rmsnorm_router(x, g, w_router)
    dest, tile_expert, n_used, max_tiles = _route(ei, n_e)
    hs = moe_dispatch(hp, dest, max_tiles * MOE_ROWS)
    act = moe_glu(hs, wg, wu, layer, tile_expert, n_used)
    ys = moe_down(act, wd, layer, tile_expert, n_used)
    return moe_combine(x, ys, dest, ew)


def _mixer(x, h, l, w_in, conv_w, w_a_out, pool_w, pool_scale, w_p_out, lam, subln, w_c_out, w_o,
           lambda_init, bsz, s):
    n, d = x.shape
    cw, pw, aw = conv_w.shape[2], pool_scale.shape[1], w_c_out.shape[1]
    proj = matmul(h, w_in, l, BF16)
    proj3 = proj.reshape(bsz, s, proj.shape[1])
    ya = short_conv(proj3, conv_w[l]).reshape(n, cw)
    yp = pool_mix(proj3, 3 * cw, pool_w[l], pool_scale[l]).reshape(n, pw)
    yc = diff_attention(proj3, 3 * cw + pw, lam[l], subln[l], lambda_init).reshape(n, aw)
    merged = merge(ya, yp, yc, w_a_out, w_p_out, w_c_out, l, proj, 3 * cw + pw + 3 * aw)
    return matmul(merged, w_o, l, F32, res=x, tm=1024)


def kernel(x, norm_mix, w_in, conv_w, w_a_out, pool_w, pool_scale, w_p_out, lam, subln, w_c_out, w_o,
           norm_ffn, ffn_w_gate, ffn_w_up, ffn_w_down, w_router, moe_w_gate, moe_w_up, moe_w_down,
           norm_final):
    bsz, s, d = x.shape
    depth = norm_mix.shape[0]
    n = bsz * s
    x = x.reshape(n, d)
    for l in range(depth):
        lambda_init = 0.8 - 0.6 * math.exp(-0.3 * l)
        h = rmsnorm(x, norm_mix[l], BF16)
        x = _mixer(x, h, l, w_in, conv_w, w_a_out, pool_w, pool_scale, w_p_out, lam, subln, w_c_out, w_o,
                   lambda_init, bsz, s)
        j = l // 2
        if l % 2 == 0:
            h = rmsnorm(x, norm_ffn[l], BF16)
            act = glu_in(h, ffn_w_gate, ffn_w_up, j)
            x = matmul_acc(act, ffn_w_down, j, x)
        else:
            x = moe_ffn(x, norm_ffn[l], w_router[j], moe_w_gate, moe_w_up, moe_w_down, j)
    return rmsnorm(x, norm_final, F32).reshape(bsz, s, d)
```

```python
import functools
import math

import jax
import jax.numpy as jnp
from jax import lax
from jax.experimental import pallas as pl
from jax.experimental.pallas import tpu as pltpu

F32 = jnp.float32
BF16 = jnp.bfloat16
U32 = jnp.uint32

NORM_EPS = 1e-6
SUBLN_EPS = 1e-5
CONV_K = 3
POOL_WINDOWS = (2, 4, 8, 16)
DA_HEADS = 8
DA_HEAD_DIM = 128
N_BRANCHES = 3
TOP_K = 2

LANES = 128
BF16_ROWS = 16
MXU_EDGE = 256
VMEM_BYTES = 64 * 1024 * 1024
VMEM_LIMIT = VMEM_BYTES - 8 * 1024 * 1024

HALO = BF16_ROWS
assert HALO >= max(POOL_WINDOWS) and HALO >= CONV_K
LOG2E = math.log2(math.e)
HI16 = 0xFFFF0000


def _tile(dim, pref, unit=LANES):
    if dim <= pref:
        return dim
    t = (pref // unit) * unit
    while t > unit and dim % t:
        t -= unit
    assert dim % t == 0, (dim, pref, unit)
    return t


def _params(sem):
    return pltpu.CompilerParams(dimension_semantics=sem, vmem_limit_bytes=VMEM_LIMIT)


def _resident(block_shape, index_map):
    return pl.BlockSpec(block_shape, index_map, pipeline_mode=pl.Buffered(1))


def _rms(x, g):
    ms = jnp.mean(x * x, axis=-1, keepdims=True)
    return x * lax.rsqrt(ms + NORM_EPS) * g


def _rmsnorm_body(x_ref, g_ref, o_ref):
    o_ref[...] = _rms(x_ref[...], g_ref[...]).astype(o_ref.dtype)


def rmsnorm(x, g, out_dtype):
    n, d = x.shape
    tr = _tile(n, 256, 8)
    return pl.pallas_call(
        _rmsnorm_body,
        grid=(n // tr,),
        in_specs=[pl.BlockSpec((tr, d), lambda i: (i, 0)),
                  pl.BlockSpec((1, d), lambda i: (0, 0))],
        out_specs=pl.BlockSpec((tr, d), lambda i: (i, 0)),
        out_shape=jax.ShapeDtypeStruct((n, d), out_dtype),
        compiler_params=_params(("parallel",)),
        name="rmsnorm",
    )(x, g.reshape(1, d))


def _pack_halves(h):
    half = h.shape[1] // 2
    lo = lax.bitcast_convert_type(h[:, :half].astype(BF16).astype(F32), U32)
    hi = lax.bitcast_convert_type(h[:, half:].astype(BF16).astype(F32), U32)
    return (lo >> 16) | (hi & jnp.uint32(HI16))


def _unpack_halves(p):
    lo = lax.bitcast_convert_type(p << 16, F32).astype(BF16)
    hi = lax.bitcast_convert_type(p & jnp.uint32(HI16), F32).astype(BF16)
    return lo, hi


def _rmsnorm_router_body(x_ref, g_ref, wr_ref, hp_ref, ei_ref, ew_ref, *, n_experts):
    h = _rms(x_ref[...], g_ref[...])
    hp_ref[...] = _pack_halves(h)
    logits = jnp.dot(h, wr_ref[...], preferred_element_type=F32, precision=lax.Precision.HIGHEST)
    lane = lax.broadcasted_iota(jnp.int32, logits.shape, 1)
    neg = jnp.float32(-jnp.inf)
    logits = jnp.where(lane < n_experts, logits, neg)
    m1 = jnp.max(logits, axis=-1, keepdims=True)
    i1 = jnp.min(jnp.where(logits == m1, lane, LANES), axis=-1, keepdims=True)
    rest = jnp.where(lane == i1, neg, logits)
    m2 = jnp.max(rest, axis=-1, keepdims=True)
    i2 = jnp.min(jnp.where(rest == m2, lane, LANES), axis=-1, keepdims=True)
    e2 = jnp.exp(m2 - m1)
    w1 = 1.0 / (1.0 + e2)
    w2 = e2 / (1.0 + e2)
    ei_ref[...] = jnp.where(lane == 0, i1, jnp.where(lane == 1, i2, 0))
    ew_ref[...] = jnp.where(lane == 0, w1, jnp.where(lane == 1, w2, 0.0))


def rmsnorm_router(x, g, w_router):
    n, d = x.shape
    n_experts = w_router.shape[1]
    wr = jnp.pad(w_router, ((0, 0), (0, LANES - n_experts)))
    tr = _tile(n, 256, 8)
    return pl.pallas_call(
        functools.partial(_rmsnorm_router_body, n_experts=n_experts),
        grid=(n // tr,),
        in_specs=[pl.BlockSpec((tr, d), lambda i: (i, 0)),
                  pl.BlockSpec((1, d), lambda i: (0, 0)),
                  pl.BlockSpec((d, LANES), lambda i: (0, 0))],
        out_specs=[pl.BlockSpec((tr, d // 2), lambda i: (i, 0)),
                   pl.BlockSpec((tr, LANES), lambda i: (i, 0)),
                   pl.BlockSpec((tr, LANES), lambda i: (i, 0))],
        out_shape=[jax.ShapeDtypeStruct((n, d // 2), U32),
                   jax.ShapeDtypeStruct((n, LANES), jnp.int32),
                   jax.ShapeDtypeStruct((n, LANES), F32)],
        compiler_params=_params(("parallel",)),
        name="rmsnorm_router",
    )(x, g.reshape(1, d), wr)


def _mm_body(a_ref, w_ref, *rest, has_res):
    r = jnp.dot(a_ref[...], w_ref[...].astype(BF16), preferred_element_type=F32)
    if has_res:
        r = r + rest[0][...]
    rest[-1][...] = r.astype(rest[-1].dtype)


def matmul(a, w, layer, out_dtype, res=None, tm=2048, tn=512):
    m, kd = a.shape
    n = w.shape[2]
    tm, tn = _tile(m, tm), _tile(n, tn)
    in_specs = [_resident((tm, kd), lambda i, j: (i, 0)),
                pl.BlockSpec((None, kd, tn), lambda i, j: (layer, 0, j))]
    args = [a, w]
    if res is not None:
        in_specs.append(pl.BlockSpec((tm, tn), lambda i, j: (i, j)))
        args.append(res)
    return pl.pallas_call(
        functools.partial(_mm_body, has_res=res is not None),
        grid=(m // tm, n // tn),
        in_specs=in_specs,
        out_specs=pl.BlockSpec((tm, tn), lambda i, j: (i, j)),
        out_shape=jax.ShapeDtypeStruct((m, n), out_dtype),
        compiler_params=_params(("parallel", "arbitrary")),
        name="matmul_res" if res is not None else "matmul",
    )(*args)


def _mm_acc_body(a_ref, w_ref, res_ref, o_ref):
    @pl.when(pl.program_id(2) == 0)
    def _():
        o_ref[...] = res_ref[...]

    o_ref[...] += jnp.dot(a_ref[...], w_ref[...].astype(BF16), preferred_element_type=F32)


def matmul_acc(a, w, layer, res, tm=2048, tn=512, tk=1792):
    m, kd = a.shape
    n = w.shape[2]
    tm, tn, tk = _tile(m, tm), _tile(n, tn, MXU_EDGE), _tile(kd, tk, MXU_EDGE)
    return pl.pallas_call(
        _mm_acc_body,
        grid=(m // tm, n // tn, kd // tk),
        in_specs=[pl.BlockSpec((tm, tk), lambda i, j, k: (i, k)),
                  pl.BlockSpec((None, tk, tn), lambda i, j, k: (layer, k, j)),
                  pl.BlockSpec((tm, tn), lambda i, j, k: (i, j))],
        out_specs=pl.BlockSpec((tm, tn), lambda i, j, k: (i, j)),
        out_shape=jax.ShapeDtypeStruct((m, n), F32),
        compiler_params=_params(("parallel", "parallel", "arbitrary")),
        name="matmul_acc",
    )(a, w, res)


def _silu_mul(g, u):
    return g * jax.nn.sigmoid(g) * u


def _glu_body(a_ref, wg_ref, wu_ref, o_ref):
    a = a_ref[...]
    g = jnp.dot(a, wg_ref[...].astype(BF16), preferred_element_type=F32)
    u = jnp.dot(a, wu_ref[...].astype(BF16), preferred_element_type=F32)
    o_ref[...] = _silu_mul(g, u).astype(o_ref.dtype)


def glu_in(a, wg, wu, layer, tm=2048, tn=256):
    m, kd = a.shape
    f = wg.shape[2]
    tm, tn = _tile(m, tm), _tile(f, tn, MXU_EDGE)
    w_spec = pl.BlockSpec((None, kd, tn), lambda i, j: (layer, 0, j))
    return pl.pallas_call(
        _glu_body,
        grid=(m // tm, f // tn),
        in_specs=[_resident((tm, kd), lambda i, j: (i, 0)), w_spec, w_spec],
        out_specs=pl.BlockSpec((tm, tn), lambda i, j: (i, j)),
        out_shape=jax.ShapeDtypeStruct((m, f), BF16),
        compiler_params=_params(("parallel", "arbitrary")),
        name="glu_in",
    )(a, wg, wu)


MERGE_ROWS = 512


def _merge_body(a_ref, p_ref, c_ref, wa_ref, wp_ref, wc_ref, g0_ref, g1_ref, g2_ref, o_ref):
    branches = ((a_ref, wa_ref[...].astype(BF16), g0_ref),
                (p_ref, wp_ref[...].astype(BF16), g1_ref),
                (c_ref, wc_ref[...].astype(BF16), g2_ref))
    rc = min(MERGE_ROWS, o_ref.shape[0])
    for r in range(o_ref.shape[0] // rc):
        rows = slice(r * rc, (r + 1) * rc)
        acc = None
        for x_ref, w, g_ref in branches:
            y = jax.nn.sigmoid(g_ref[rows].astype(F32)) * jnp.dot(x_ref[rows], w, preferred_element_type=F32)
            acc = y if acc is None else acc + y
        o_ref[rows] = acc.astype(o_ref.dtype)


def merge(ya, yp, yc, wa, wp, wc, layer, proj, gate_col, tm=2048, tn=256):
    m = ya.shape[0]
    d = wa.shape[2]
    tm, tn = _tile(m, tm), _tile(d, tn)
    assert gate_col % tn == 0
    gb, nd = gate_col // tn, d // tn

    def x_spec(x):
        return _resident((tm, x.shape[1]), lambda i, j: (i, 0))

    def w_spec(w):
        return pl.BlockSpec((None, w.shape[1], tn), lambda i, j: (layer, 0, j))

    def g_spec(b):
        return pl.BlockSpec((tm, tn), lambda i, j: (i, gb + b * nd + j))

    return pl.pallas_call(
        _merge_body,
        grid=(m // tm, nd),
        in_specs=[x_spec(ya), x_spec(yp), x_spec(yc), w_spec(wa), w_spec(wp), w_spec(wc),
                  g_spec(0), g_spec(1), g_spec(2)],
        out_specs=pl.BlockSpec((tm, tn), lambda i, j: (i, j)),
        out_shape=jax.ShapeDtypeStruct((m, d), BF16),
        compiler_params=_params(("parallel", "parallel")),
        name="merge",
    )(ya, yp, yc, wa, wp, wc, proj, proj, proj)


def _with_halo(prev, cur, first):
    prev = jnp.where(first, 0.0, prev)
    return jnp.concatenate([prev, cur], axis=0)


def _conv_body(u_ref, b_ref, c_ref, up_ref, cp_ref, w_ref, o_ref):
    first = pl.program_id(1) == 0
    z = c_ref[...].astype(F32) * u_ref[...].astype(F32)
    zp = cp_ref[...].astype(F32) * up_ref[...].astype(F32)
    ze = _with_halo(zp, z, first)
    z1 = pltpu.roll(ze, 1, 0)[HALO:]
    z2 = pltpu.roll(ze, 2, 0)[HALO:]
    w = w_ref[...]
    zc = w[0:1] * z2 + w[1:2] * z1 + w[2:3] * z
    o_ref[...] = (b_ref[...].astype(F32) * zc).astype(o_ref.dtype)


def short_conv(proj, conv_w, ts=512, cw=256):
    bsz, s, _ = proj.shape
    width = conv_w.shape[1]
    ts, cw = _tile(s, ts, HALO), _tile(width, cw)
    nc, hb = width // cw, ts // HALO

    def cur(off):
        return pl.BlockSpec((None, ts, cw), lambda b, i, j: (b, i, off * nc + j))

    def prev(off):
        return pl.BlockSpec((None, HALO, cw), lambda b, i, j: (b, jnp.maximum(i * hb - 1, 0), off * nc + j))

    return pl.pallas_call(
        _conv_body,
        grid=(bsz, s // ts, nc),
        in_specs=[cur(0), cur(1), cur(2), prev(0), prev(2),
                  pl.BlockSpec((CONV_K, cw), lambda b, i, j: (0, j))],
        out_specs=pl.BlockSpec((None, ts, cw), lambda b, i, j: (b, i, j)),
        out_shape=jax.ShapeDtypeStruct((bsz, s, width), BF16),
        compiler_params=_params(("parallel", "parallel", "parallel")),
        name="short_conv",
    )(proj, proj, proj, proj, proj, conv_w)


def _pool_body(u_ref, up_ref, pw_ref, ps_ref, o_ref, *, ts):
    i, g = pl.program_id(1), pl.program_id(2)
    u = u_ref[...].astype(F32)
    ue = _with_halo(up_ref[...].astype(F32), u, i == 0)
    t1 = i * ts + lax.broadcasted_iota(jnp.int32, u.shape, 0) + 1
    pooled = jnp.zeros_like(u)
    s, span = ue, 1
    for gi, w in enumerate(POOL_WINDOWS):
        assert w == 2 * span
        s = s + pltpu.roll(s, span, 0)
        span = w
        mean = s[HALO:] / jnp.minimum(t1, w).astype(F32)
        pooled = jnp.where(g == gi, mean, pooled)
    mixed = (pooled - u).astype(BF16)
    y = jnp.dot(mixed, pw_ref[...].astype(BF16), preferred_element_type=F32)
    o_ref[...] = (y * ps_ref[...]).astype(o_ref.dtype)


def pool_mix(proj, col, pool_w, pool_scale, ts=512):
    bsz, s, _ = proj.shape
    n_g, gd, _ = pool_w.shape
    assert n_g == len(POOL_WINDOWS) and col % gd == 0
    ts = _tile(s, ts, HALO)
    cb, hb = col // gd, ts // HALO
    return pl.pallas_call(
        functools.partial(_pool_body, ts=ts),
        grid=(bsz, s // ts, n_g),
        in_specs=[pl.BlockSpec((None, ts, gd), lambda b, i, g: (b, i, cb + g)),
                  pl.BlockSpec((None, HALO, gd), lambda b, i, g: (b, jnp.maximum(i * hb - 1, 0), cb + g)),
                  pl.BlockSpec((None, gd, gd), lambda b, i, g: (g, 0, 0)),
                  pl.BlockSpec((1, gd), lambda b, i, g: (0, g))],
        out_specs=pl.BlockSpec((None, ts, gd), lambda b, i, g: (b, i, g)),
        out_shape=jax.ShapeDtypeStruct((bsz, s, n_g * gd), BF16),
        compiler_params=_params(("parallel", "parallel", "parallel")),
        name="pool_mix",
    )(proj, proj, pool_w, pool_scale.reshape(1, n_g * gd))


def _attn_body(lam_ref, g_ref, q_ref, k_ref, v_ref, o_ref, s_ref, m_ref, l_ref, acc_ref, *, tq, lambda_init):
    dh = DA_HEAD_DIM
    i = pl.program_id(2)
    maps = range(2)
    slabs = [slice(b * LANES, (b + 1) * LANES) for b in range(tq // LANES)]
    q = (q_ref[...].astype(F32) * (dh ** -0.5 * LOG2E)).astype(BF16)

    def rows(j):
        return pl.ds(pl.multiple_of(j * tq, tq), tq)

    def scores(j, c):
        cols = slice(c * dh, (c + 1) * dh)
        return lax.dot_general(q[:, cols], k_ref[rows(j), cols], (((1,), (1,)), ((), ())),
                               preferred_element_type=F32)

    def keep(j, c, s):
        s_ref[c, j] = s
        m = m_ref[c]
        for sl in slabs:
            m = jnp.maximum(m, s[:, sl])
        m_ref[c] = m

    def pass1(j, carry):
        for c in maps:
            keep(j, c, scores(j, c))
        return carry

    m_ref[...] = jnp.full_like(m_ref, -jnp.inf)
    lax.fori_loop(0, i, pass1, 0)
    row = lax.broadcasted_iota(jnp.int32, (tq, tq), 0)
    col = lax.broadcasted_iota(jnp.int32, (tq, tq), 1)
    for c in maps:
        keep(i, c, jnp.where(col <= row, scores(i, c), -jnp.inf))
    mb = [jnp.broadcast_to(jnp.max(m_ref[c], axis=-1, keepdims=True), (tq, LANES)) for c in maps]

    def pass2(j, carry):
        p_maps = []
        for c in maps:
            s = s_ref[c, j]
            l = l_ref[c]
            ps = []
            for sl in slabs:
                p = jnp.exp2(s[:, sl] - mb[c])
                l = l + p
                ps.append(p.astype(BF16))
            l_ref[c] = l
            p_maps.append(jnp.concatenate(ps, axis=1))
        acc_ref[...] += jnp.dot(jnp.concatenate(p_maps, axis=0), v_ref[rows(j), :], preferred_element_type=F32)
        return carry

    l_ref[...] = jnp.zeros_like(l_ref)
    acc_ref[...] = jnp.zeros_like(acc_ref)
    lax.fori_loop(0, i + 1, pass2, 0)
    outs = [acc_ref[c * tq:(c + 1) * tq] / jnp.sum(l_ref[c], axis=-1, keepdims=True) for c in maps]

    lp = lam_ref[...]
    lam = (jnp.exp(jnp.sum(lp[0:1] * lp[1:2], axis=-1, keepdims=True))
           - jnp.exp(jnp.sum(lp[2:3] * lp[3:4], axis=-1, keepdims=True)) + lambda_init)
    o = outs[0] - lam * outs[1]
    o = o * lax.rsqrt(jnp.mean(o * o, axis=-1, keepdims=True) + SUBLN_EPS)
    o_ref[...] = (o * g_ref[...] * (1.0 - lambda_init)).astype(o_ref.dtype)


def diff_attention(proj, col, lam, subln_g, lambda_init, tq=512):
    bsz, s, _ = proj.shape
    hw = 2 * DA_HEAD_DIM
    assert col % hw == 0
    tq = _tile(s, tq)
    qb = col // hw
    kb, vb = qb + DA_HEADS, qb + 2 * DA_HEADS
    return pl.pallas_call(
        functools.partial(_attn_body, tq=tq, lambda_init=lambda_init),
        grid=(bsz, DA_HEADS, s // tq),
        in_specs=[pl.BlockSpec((4, DA_HEAD_DIM), lambda b, h, i: (0, 0)),
                  pl.BlockSpec((1, hw), lambda b, h, i: (0, 0)),
                  pl.BlockSpec((None, tq, hw), lambda b, h, i: (b, i, qb + h)),
                  pl.BlockSpec((None, s, hw), lambda b, h, i: (b, 0, kb + h)),
                  pl.BlockSpec((None, s, hw), lambda b, h, i: (b, 0, vb + h))],
        out_specs=pl.BlockSpec((None, tq, hw), lambda b, h, i: (b, i, h)),
        out_shape=jax.ShapeDtypeStruct((bsz, s, DA_HEADS * hw), BF16),
        scratch_shapes=[pltpu.VMEM((2, s // tq, tq, tq), F32), pltpu.VMEM((2, tq, LANES), F32),
                        pltpu.VMEM((2, tq, LANES), F32), pltpu.VMEM((2 * tq, hw), F32)],
        compiler_params=_params(("parallel", "parallel", "arbitrary")),
        name="diff_attention",
    )(lam, subln_g.reshape(1, hw), proj, proj, proj)


MOE_ROWS = 512
MOE_TOKENS = 256


def _route(ei, n_e):
    n = ei.shape[0]
    e = ei[:, :TOP_K]
    onehot = (e[:, :, None] == jnp.arange(n_e, dtype=jnp.int32)).astype(jnp.int32).reshape(n * TOP_K, n_e)
    incl = jnp.cumsum(onehot, axis=0)
    rank = jnp.sum((incl - onehot) * onehot, axis=1)
    tiles = (incl[-1] + MOE_ROWS - 1) // MOE_ROWS
    tile_end = jnp.cumsum(tiles)
    start = (tile_end - tiles) * MOE_ROWS
    dest = (jnp.sum(onehot * start[None, :], axis=1) + rank).reshape(n, TOP_K)
    max_tiles = (n * TOP_K) // MOE_ROWS + n_e
    n_used = tile_end[-1:]
    t = jnp.minimum(jnp.arange(max_tiles, dtype=jnp.int32), n_used[0] - 1)
    tile_expert = jnp.sum((t[:, None] >= tile_end[None, :]).astype(jnp.int32), axis=1)
    return dest.astype(jnp.int32), tile_expert.astype(jnp.int32), n_used.astype(jnp.int32), max_tiles


def _dispatch_body(dest_ref, hp_ref, init_ref, hs_ref, sem, *, tt):
    del init_ref

    def row_copy(t, k):
        return pltpu.make_async_copy(hp_ref.at[pl.ds(t, 1)], hs_ref.at[pl.ds(dest_ref[0, TOP_K * t + k], 1)], sem)

    def start(t, carry):
        for k in range(TOP_K):
            row_copy(t, k).start()
        return carry

    def wait(t, carry):
        for k in range(TOP_K):
            row_copy(t, k).wait()
        return carry

    lax.fori_loop(0, tt, start, 0)
    lax.fori_loop(0, tt, wait, 0)


def moe_dispatch(hp, dest, n_rows):
    n, w = hp.shape
    tt = _tile(n, MOE_TOKENS, 8)
    return pl.pallas_call(
        functools.partial(_dispatch_body, tt=tt),
        grid=(n // tt,),
        in_specs=[pl.BlockSpec((None, 1, TOP_K * tt), lambda i: (i, 0, 0), memory_space=pltpu.SMEM),
                  pl.BlockSpec((tt, w), lambda i: (i, 0)),
                  pl.BlockSpec(memory_space=pl.ANY)],
        out_specs=pl.BlockSpec(memory_space=pl.ANY),
        out_shape=jax.ShapeDtypeStruct((n_rows, w), U32),
        scratch_shapes=[pltpu.SemaphoreType.DMA(())],
        input_output_aliases={2: 0},
        compiler_params=_params(("arbitrary",)),
        name="moe_dispatch",
    )(dest.reshape(n // tt, 1, TOP_K * tt), hp, jnp.zeros((n_rows, w), U32))


def _zero_unused_tile(nu_ref, o_ref):
    @pl.when(pl.program_id(1) >= nu_ref[0])
    def _():
        o_ref[...] = jnp.zeros_like(o_ref)


def _moe_glu_body(te_ref, nu_ref, a_ref, wg_ref, wu_ref, o_ref, wgb_ref, wub_ref):
    r = pl.program_id(1)
    _zero_unused_tile(nu_ref, o_ref)

    @pl.when((r == 0) | (te_ref[r] != te_ref[jnp.maximum(r - 1, 0)]))
    def _():
        wgb_ref[...] = wg_ref[...].astype(BF16)
        wub_ref[...] = wu_ref[...].astype(BF16)

    @pl.when(r < nu_ref[0])
    def _():
        lo, hi = _unpack_halves(a_ref[...])
        half = lo.shape[1]

        def proj(w_ref):
            return (jnp.dot(lo, w_ref[:half], preferred_element_type=F32)
                    + jnp.dot(hi, w_ref[half:], preferred_element_type=F32))

        o_ref[...] = _silu_mul(proj(wgb_ref), proj(wub_ref)).astype(o_ref.dtype)


def moe_glu(hs, wg, wu, layer, tile_expert, n_used, tn=256):
    rows, half = hs.shape
    f = wg.shape[3]
    tn = _tile(f, tn, MXU_EDGE)
    n_tiles = rows // MOE_ROWS

    def row_tile(r, nu):
        return jnp.minimum(r, nu[0] - 1)

    w_spec = pl.BlockSpec((None, None, 2 * half, tn), lambda j, r, te, nu: (layer, te[r], 0, j))
    return pl.pallas_call(
        _moe_glu_body,
        grid_spec=pltpu.PrefetchScalarGridSpec(
            num_scalar_prefetch=2,
            grid=(f // tn, n_tiles),
            in_specs=[pl.BlockSpec((MOE_ROWS, half), lambda j, r, te, nu: (row_tile(r, nu), 0)), w_spec, w_spec],
            out_specs=pl.BlockSpec((MOE_ROWS, tn), lambda j, r, te, nu: (r, j)),
            scratch_shapes=[pltpu.VMEM((2 * half, tn), BF16), pltpu.VMEM((2 * half, tn), BF16)]),
        out_shape=jax.ShapeDtypeStruct((rows, f), BF16),
        compiler_params=_params(("arbitrary", "arbitrary")),
        name="moe_glu",
    )(tile_expert, n_used, hs, wg, wu)


def _moe_down_body(te_ref, nu_ref, a_ref, w_ref, o_ref):
    del te_ref
    _zero_unused_tile(nu_ref, o_ref)

    @pl.when(pl.program_id(1) < nu_ref[0])
    def _():
        o_ref[...] = jnp.dot(a_ref[...], w_ref[...].astype(BF16), preferred_element_type=F32)


def moe_down(act, wd, layer, tile_expert, n_used, tn=1024):
    rows, f = act.shape
    d = wd.shape[3]
    tn = _tile(d, tn)
    n_tiles = rows // MOE_ROWS

    def row_tile(r, nu):
        return jnp.minimum(r, nu[0] - 1)

    return pl.pallas_call(
        _moe_down_body,
        grid_spec=pltpu.PrefetchScalarGridSpec(
            num_scalar_prefetch=2,
            grid=(d // tn, n_tiles),
            in_specs=[pl.BlockSpec((MOE_ROWS, f), lambda j, r, te, nu: (row_tile(r, nu), 0)),
                      pl.BlockSpec((None, None, f, tn), lambda j, r, te, nu: (layer, te[r], 0, j))],
            out_specs=pl.BlockSpec((MOE_ROWS, tn), lambda j, r, te, nu: (r, j))),
        out_shape=jax.ShapeDtypeStruct((rows, d), F32),
        compiler_params=_params(("arbitrary", "arbitrary")),
        name="moe_down",
    )(tile_expert, n_used, act, wd)


def _combine_body(dest_ref, next_ref, ew_ref, x_ref, *rest, tt, has_norm):
    if has_norm:
        g_ref, ys_ref, o_ref, ybuf, sem = rest
    else:
        ys_ref, o_ref, ybuf, sem = rest
    i, n_steps = pl.program_id(0), pl.num_programs(0)
    slot = i % 2

    def row_copy(d_ref, sl, t, k):
        return pltpu.make_async_copy(ys_ref.at[pl.ds(d_ref[0, TOP_K * t + k], 1)],
                                     ybuf.at[sl, k, pl.ds(t, 1)], sem.at[sl])

    def start_all(d_ref, sl):
        def start(t, carry):
            for k in range(TOP_K):
                row_copy(d_ref, sl, t, k).start()
            return carry

        lax.fori_loop(0, tt, start, 0)

    @pl.when(i == 0)
    def _():
        start_all(dest_ref, slot)

    @pl.when(i + 1 < n_steps)
    def _():
        start_all(next_ref, 1 - slot)

    def wait(t, carry):
        for k in range(TOP_K):
            row_copy(dest_ref, slot, t, k).wait()
        return carry

    lax.fori_loop(0, tt, wait, 0)
    ew = ew_ref[...]
    y = x_ref[...] + ew[:, 0:1] * ybuf[slot, 0] + ew[:, 1:2] * ybuf[slot, 1]
    o_ref[...] = _rms(y, g_ref[...]) if has_norm else y


def moe_combine(x, ys, dest, ew, norm_g=None):
    n, d = x.shape
    tt = _tile(n, MOE_TOKENS, 8)
    n_steps = n // tt
    dest = dest.reshape(n_steps, 1, TOP_K * tt)

    def d_spec(ahead):
        return pl.BlockSpec((None, 1, TOP_K * tt), lambda i: (jnp.minimum(i + ahead, n_steps - 1), 0, 0),
                            memory_space=pltpu.SMEM)

    in_specs = [d_spec(0), d_spec(1),
                pl.BlockSpec((tt, LANES), lambda i: (i, 0)),
                pl.BlockSpec((tt, d), lambda i: (i, 0))]
    args = [dest, dest, ew, x]
    if norm_g is not None:
        in_specs.append(pl.BlockSpec((1, d), lambda i: (0, 0)))
        args.append(norm_g.reshape(1, d))
    return pl.pallas_call(
        functools.partial(_combine_body, tt=tt, has_norm=norm_g is not None),
        grid=(n_steps,),
        in_specs=in_specs + [pl.BlockSpec(memory_space=pl.ANY)],
        out_specs=pl.BlockSpec((tt, d), lambda i: (i, 0)),
        out_shape=jax.ShapeDtypeStruct((n, d), F32),
        scratch_shapes=[pltpu.VMEM((2, TOP_K, tt, d), F32), pltpu.SemaphoreType.DMA((2,))],
        compiler_params=_params(("arbitrary",)),
        name="moe_combine",
    )(*args, ys)


def moe_ffn(x, g, w_router, wg, wu, wd, layer, norm_g=None):
    n_e = wg.shape[1]
    hp, ei, ew = rmsnorm_router(x, g, w_router)
    dest, tile_expert, n_used, max_tiles = _route(ei, n_e)
    hs = moe_dispatch(hp, dest, max_tiles * MOE_ROWS)
    act = moe_glu(hs, wg, wu, layer, tile_expert, n_used)
    ys = moe_down(act, wd, layer, tile_expert, n_used)
    return moe_combine(x, ys, dest, ew, norm_g)


def _mixer(x, h, l, w_in, conv_w, w_a_out, pool_w, pool_scale, w_p_out, lam, subln, w_c_out, w_o,
           lambda_init, bsz, s):
    n, d = x.shape
    cw, pw, aw = conv_w.shape[2], pool_scale.shape[1], w_c_out.shape[1]
    proj = matmul(h, w_in, l, BF16)
    proj3 = proj.reshape(bsz, s, proj.shape[1])
    ya = short_conv(proj3, conv_w[l]).reshape(n, cw)
    yp = pool_mix(proj3, 3 * cw, pool_w[l], pool_scale[l]).reshape(n, pw)
    yc = diff_attention(proj3, 3 * cw + pw, lam[l], subln[l], lambda_init).reshape(n, aw)
    merged = merge(ya, yp, yc, w_a_out, w_p_out, w_c_out, l, proj, 3 * cw + pw + 3 * aw)
    return matmul(merged, w_o, l, F32, res=x, tn=256)


def kernel(x, norm_mix, w_in, conv_w, w_a_out, pool_w, pool_scale, w_p_out, lam, subln, w_c_out, w_o,
           norm_ffn, ffn_w_gate, ffn_w_up, ffn_w_down, w_router, moe_w_gate, moe_w_up, moe_w_down,
           norm_final):
    bsz, s, d = x.shape
    depth = norm_mix.shape[0]
    n = bsz * s
    x = x.reshape(n, d)
    for l in range(depth):
        lambda_init = 0.8 - 0.6 * math.exp(-0.3 * l)
        h = rmsnorm(x, norm_mix[l], BF16)
        x = _mixer(x, h, l, w_in, conv_w, w_a_out, pool_w, pool_scale, w_p_out, lam, subln, w_c_out, w_o,
                   lambda_init, bsz, s)
        j = l // 2
        if l % 2 == 0:
            h = rmsnorm(x, norm_ffn[l], BF16)
            act = glu_in(h, ffn_w_gate, ffn_w_up, j)
            x = matmul_acc(act, ffn_w_down, j, x)
        elif l < depth - 1:
            x = moe_ffn(x, norm_ffn[l], w_router[j], moe_w_gate, moe_w_up, moe_w_down, j)
        else:
            return moe_ffn(x, norm_ffn[l], w_router[j], moe_w_gate, moe_w_up, moe_w_down, j,
                           norm_g=norm_final).reshape(bsz, s, d)
    return rmsnorm(x, norm_final, F32).reshape(bsz, s, d)
```

```python
import functools
import math

import jax
import jax.numpy as jnp
from jax import lax
from jax.experimental import pallas as pl
from jax.experimental.pallas import tpu as pltpu

F32 = jnp.float32
BF16 = jnp.bfloat16
U32 = jnp.uint32

NORM_EPS = 1e-6
SUBLN_EPS = 1e-5
CONV_K = 3
POOL_WINDOWS = (2, 4, 8, 16)
DA_HEADS = 8
DA_HEAD_DIM = 128
N_BRANCHES = 3
TOP_K = 2

LANES = 128
BF16_ROWS = 16
MXU_EDGE = 256
VMEM_BYTES = 64 * 1024 * 1024
VMEM_LIMIT = VMEM_BYTES - 8 * 1024 * 1024

HALO = BF16_ROWS
assert HALO >= max(POOL_WINDOWS) and HALO >= CONV_K
LOG2E = math.log2(math.e)
HI16 = 0xFFFF0000


def _tile(dim, pref, unit=LANES):
    if dim <= pref:
        return dim
    t = (pref // unit) * unit
    while t > unit and dim % t:
        t -= unit
    assert dim % t == 0, (dim, pref, unit)
    return t


def _params(sem):
    return pltpu.CompilerParams(dimension_semantics=sem, vmem_limit_bytes=VMEM_LIMIT)


def _resident(block_shape, index_map):
    return pl.BlockSpec(block_shape, index_map, pipeline_mode=pl.Buffered(1))


def _rms(x, g):
    ms = jnp.mean(x * x, axis=-1, keepdims=True)
    return x * lax.rsqrt(ms + NORM_EPS) * g


def _rmsnorm_body(x_ref, g_ref, o_ref):
    o_ref[...] = _rms(x_ref[...], g_ref[...]).astype(o_ref.dtype)


def rmsnorm(x, g, out_dtype):
    n, d = x.shape
    tr = _tile(n, 256, 8)
    return pl.pallas_call(
        _rmsnorm_body,
        grid=(n // tr,),
        in_specs=[pl.BlockSpec((tr, d), lambda i: (i, 0)),
                  pl.BlockSpec((1, d), lambda i: (0, 0))],
        out_specs=pl.BlockSpec((tr, d), lambda i: (i, 0)),
        out_shape=jax.ShapeDtypeStruct((n, d), out_dtype),
        compiler_params=_params(("parallel",)),
        name="rmsnorm",
    )(x, g.reshape(1, d))


def _pack_halves(h):
    half = h.shape[1] // 2
    lo = lax.bitcast_convert_type(h[:, :half].astype(BF16).astype(F32), U32)
    hi = lax.bitcast_convert_type(h[:, half:].astype(BF16).astype(F32), U32)
    return (lo >> 16) | (hi & jnp.uint32(HI16))


def _unpack_halves(p):
    lo = lax.bitcast_convert_type(p << 16, F32).astype(BF16)
    hi = lax.bitcast_convert_type(p & jnp.uint32(HI16), F32).astype(BF16)
    return lo, hi


def _split_bf16(x):
    hi = x.astype(BF16)
    return hi, (x - hi.astype(F32)).astype(BF16)


def _rmsnorm_router_body(x_ref, g_ref, wr_ref, hp_ref, ei_ref, ew_ref, *, n_experts):
    h = _rms(x_ref[...], g_ref[...])
    hp_ref[...] = _pack_halves(h)
    h_hi, h_lo = _split_bf16(h)
    w_hi, w_lo = _split_bf16(wr_ref[...])
    logits = (jnp.dot(h_hi, w_hi, preferred_element_type=F32) + jnp.dot(h_hi, w_lo, preferred_element_type=F32)
              + jnp.dot(h_lo, w_hi, preferred_element_type=F32))
    lane = lax.broadcasted_iota(jnp.int32, logits.shape, 1)
    neg = jnp.float32(-jnp.inf)
    logits = jnp.where(lane < n_experts, logits, neg)
    m1 = jnp.max(logits, axis=-1, keepdims=True)
    i1 = jnp.min(jnp.where(logits == m1, lane, LANES), axis=-1, keepdims=True)
    rest = jnp.where(lane == i1, neg, logits)
    m2 = jnp.max(rest, axis=-1, keepdims=True)
    i2 = jnp.min(jnp.where(rest == m2, lane, LANES), axis=-1, keepdims=True)
    e2 = jnp.exp(m2 - m1)
    w1 = 1.0 / (1.0 + e2)
    w2 = e2 / (1.0 + e2)
    ei_ref[...] = jnp.where(lane == 0, i1, jnp.where(lane == 1, i2, 0))
    ew_ref[...] = jnp.where(lane == 0, w1, jnp.where(lane == 1, w2, 0.0))


def rmsnorm_router(x, g, w_router):
    n, d = x.shape
    n_experts = w_router.shape[1]
    wr = jnp.pad(w_router, ((0, 0), (0, LANES - n_experts)))
    tr = _tile(n, 256, 8)
    return pl.pallas_call(
        functools.partial(_rmsnorm_router_body, n_experts=n_experts),
        grid=(n // tr,),
        in_specs=[pl.BlockSpec((tr, d), lambda i: (i, 0)),
                  pl.BlockSpec((1, d), lambda i: (0, 0)),
                  pl.BlockSpec((d, LANES), lambda i: (0, 0))],
        out_specs=[pl.BlockSpec((tr, d // 2), lambda i: (i, 0)),
                   pl.BlockSpec((tr, LANES), lambda i: (i, 0)),
                   pl.BlockSpec((tr, LANES), lambda i: (i, 0))],
        out_shape=[jax.ShapeDtypeStruct((n, d // 2), U32),
                   jax.ShapeDtypeStruct((n, LANES), jnp.int32),
                   jax.ShapeDtypeStruct((n, LANES), F32)],
        compiler_params=_params(("parallel",)),
        name="rmsnorm_router",
    )(x, g.reshape(1, d), wr)


def _mm_body(a_ref, w_ref, *rest, has_res):
    r = jnp.dot(a_ref[...], w_ref[...].astype(BF16), preferred_element_type=F32)
    if has_res:
        r = r + rest[0][...]
    rest[-1][...] = r.astype(rest[-1].dtype)


def matmul(a, w, layer, out_dtype, res=None, tm=2048, tn=512):
    m, kd = a.shape
    n = w.shape[2]
    tm, tn = _tile(m, tm), _tile(n, tn)
    in_specs = [_resident((tm, kd), lambda i, j: (i, 0)),
                pl.BlockSpec((None, kd, tn), lambda i, j: (layer, 0, j))]
    args = [a, w]
    if res is not None:
        in_specs.append(pl.BlockSpec((tm, tn), lambda i, j: (i, j)))
        args.append(res)
    return pl.pallas_call(
        functools.partial(_mm_body, has_res=res is not None),
        grid=(m // tm, n // tn),
        in_specs=in_specs,
        out_specs=pl.BlockSpec((tm, tn), lambda i, j: (i, j)),
        out_shape=jax.ShapeDtypeStruct((m, n), out_dtype),
        compiler_params=_params(("parallel", "arbitrary")),
        name="matmul_res" if res is not None else "matmul",
    )(*args)


def _mm_acc_body(a_ref, w_ref, res_ref, o_ref):
    @pl.when(pl.program_id(2) == 0)
    def _():
        o_ref[...] = res_ref[...]

    o_ref[...] += jnp.dot(a_ref[...], w_ref[...].astype(BF16), preferred_element_type=F32)


def matmul_acc(a, w, layer, res, tm=2048, tn=512, tk=1792):
    m, kd = a.shape
    n = w.shape[2]
    tm, tn, tk = _tile(m, tm), _tile(n, tn, MXU_EDGE), _tile(kd, tk, MXU_EDGE)
    return pl.pallas_call(
        _mm_acc_body,
        grid=(m // tm, n // tn, kd // tk),
        in_specs=[pl.BlockSpec((tm, tk), lambda i, j, k: (i, k)),
                  pl.BlockSpec((None, tk, tn), lambda i, j, k: (layer, k, j)),
                  pl.BlockSpec((tm, tn), lambda i, j, k: (i, j))],
        out_specs=pl.BlockSpec((tm, tn), lambda i, j, k: (i, j)),
        out_shape=jax.ShapeDtypeStruct((m, n), F32),
        compiler_params=_params(("parallel", "parallel", "arbitrary")),
        name="matmul_acc",
    )(a, w, res)


def _silu_mul(g, u):
    return g * jax.nn.sigmoid(g) * u


def _glu_body(a_ref, wg_ref, wu_ref, o_ref):
    a = a_ref[...]
    g = jnp.dot(a, wg_ref[...].astype(BF16), preferred_element_type=F32)
    u = jnp.dot(a, wu_ref[...].astype(BF16), preferred_element_type=F32)
    o_ref[...] = _silu_mul(g, u).astype(o_ref.dtype)


def glu_in(a, wg, wu, layer, tm=2048, tn=256):
    m, kd = a.shape
    f = wg.shape[2]
    tm, tn = _tile(m, tm), _tile(f, tn, MXU_EDGE)
    w_spec = pl.BlockSpec((None, kd, tn), lambda i, j: (layer, 0, j))
    return pl.pallas_call(
        _glu_body,
        grid=(m // tm, f // tn),
        in_specs=[_resident((tm, kd), lambda i, j: (i, 0)), w_spec, w_spec],
        out_specs=pl.BlockSpec((tm, tn), lambda i, j: (i, j)),
        out_shape=jax.ShapeDtypeStruct((m, f), BF16),
        compiler_params=_params(("parallel", "arbitrary")),
        name="glu_in",
    )(a, wg, wu)


MERGE_ROWS = 512


def _merge_body(a_ref, p_ref, c_ref, wa_ref, wp_ref, wc_ref, g0_ref, g1_ref, g2_ref, o_ref):
    branches = ((a_ref, wa_ref[...].astype(BF16), g0_ref),
                (p_ref, wp_ref[...].astype(BF16), g1_ref),
                (c_ref, wc_ref[...].astype(BF16), g2_ref))
    rc = min(MERGE_ROWS, o_ref.shape[0])
    for r in range(o_ref.shape[0] // rc):
        rows = slice(r * rc, (r + 1) * rc)
        acc = None
        for x_ref, w, g_ref in branches:
            y = jax.nn.sigmoid(g_ref[rows].astype(F32)) * jnp.dot(x_ref[rows], w, preferred_element_type=F32)
            acc = y if acc is None else acc + y
        o_ref[rows] = acc.astype(o_ref.dtype)


def merge(ya, yp, yc, wa, wp, wc, layer, proj, gate_col, tm=2048, tn=256):
    m = ya.shape[0]
    d = wa.shape[2]
    tm, tn = _tile(m, tm), _tile(d, tn)
    assert gate_col % tn == 0
    gb, nd = gate_col // tn, d // tn

    def x_spec(x):
        return _resident((tm, x.shape[1]), lambda i, j: (i, 0))

    def w_spec(w):
        return pl.BlockSpec((None, w.shape[1], tn), lambda i, j: (layer, 0, j))

    def g_spec(b):
        return pl.BlockSpec((tm, tn), lambda i, j: (i, gb + b * nd + j))

    return pl.pallas_call(
        _merge_body,
        grid=(m // tm, nd),
        in_specs=[x_spec(ya), x_spec(yp), x_spec(yc), w_spec(wa), w_spec(wp), w_spec(wc),
                  g_spec(0), g_spec(1), g_spec(2)],
        out_specs=pl.BlockSpec((tm, tn), lambda i, j: (i, j)),
        out_shape=jax.ShapeDtypeStruct((m, d), BF16),
        compiler_params=_params(("parallel", "parallel")),
        name="merge",
    )(ya, yp, yc, wa, wp, wc, proj, proj, proj)


def _with_halo(prev, cur, first):
    prev = jnp.where(first, 0.0, prev)
    return jnp.concatenate([prev, cur], axis=0)


def _conv_body(u_ref, b_ref, c_ref, up_ref, cp_ref, w_ref, o_ref):
    first = pl.program_id(1) == 0
    z = c_ref[...].astype(F32) * u_ref[...].astype(F32)
    zp = cp_ref[...].astype(F32) * up_ref[...].astype(F32)
    ze = _with_halo(zp, z, first)
    z1 = pltpu.roll(ze, 1, 0)[HALO:]
    z2 = pltpu.roll(ze, 2, 0)[HALO:]
    w = w_ref[...]
    zc = w[0:1] * z2 + w[1:2] * z1 + w[2:3] * z
    o_ref[...] = (b_ref[...].astype(F32) * zc).astype(o_ref.dtype)


def short_conv(proj, conv_w, ts=2048, cw=256):
    bsz, s, _ = proj.shape
    width = conv_w.shape[1]
    ts, cw = _tile(s, ts, HALO), _tile(width, cw)
    nc, hb = width // cw, ts // HALO

    def cur(off):
        return pl.BlockSpec((None, ts, cw), lambda b, i, j: (b, i, off * nc + j))

    def prev(off):
        return pl.BlockSpec((None, HALO, cw), lambda b, i, j: (b, jnp.maximum(i * hb - 1, 0), off * nc + j))

    return pl.pallas_call(
        _conv_body,
        grid=(bsz, s // ts, nc),
        in_specs=[cur(0), cur(1), cur(2), prev(0), prev(2),
                  pl.BlockSpec((CONV_K, cw), lambda b, i, j: (0, j))],
        out_specs=pl.BlockSpec((None, ts, cw), lambda b, i, j: (b, i, j)),
        out_shape=jax.ShapeDtypeStruct((bsz, s, width), BF16),
        compiler_params=_params(("parallel", "parallel", "parallel")),
        name="short_conv",
    )(proj, proj, proj, proj, proj, conv_w)


def _pool_body(u_ref, up_ref, pw_ref, ps_ref, o_ref, *, ts):
    i, g = pl.program_id(1), pl.program_id(2)
    u = u_ref[...].astype(F32)
    ue = _with_halo(up_ref[...].astype(F32), u, i == 0)
    t1 = i * ts + lax.broadcasted_iota(jnp.int32, u.shape, 0) + 1
    pooled = jnp.zeros_like(u)
    s, span = ue, 1
    for gi, w in enumerate(POOL_WINDOWS):
        assert w == 2 * span
        s = s + pltpu.roll(s, span, 0)
        span = w
        mean = s[HALO:] / jnp.minimum(t1, w).astype(F32)
        pooled = jnp.where(g == gi, mean, pooled)
    mixed = (pooled - u).astype(BF16)
    y = jnp.dot(mixed, pw_ref[...].astype(BF16), preferred_element_type=F32)
    o_ref[...] = (y * ps_ref[...]).astype(o_ref.dtype)


def pool_mix(proj, col, pool_w, pool_scale, ts=2048):
    bsz, s, _ = proj.shape
    n_g, gd, _ = pool_w.shape
    assert n_g == len(POOL_WINDOWS) and col % gd == 0
    ts = _tile(s, ts, HALO)
    cb, hb = col // gd, ts // HALO
    return pl.pallas_call(
        functools.partial(_pool_body, ts=ts),
        grid=(bsz, s // ts, n_g),
        in_specs=[pl.BlockSpec((None, ts, gd), lambda b, i, g: (b, i, cb + g)),
                  pl.BlockSpec((None, HALO, gd), lambda b, i, g: (b, jnp.maximum(i * hb - 1, 0), cb + g)),
                  pl.BlockSpec((None, gd, gd), lambda b, i, g: (g, 0, 0)),
                  pl.BlockSpec((1, gd), lambda b, i, g: (0, g))],
        out_specs=pl.BlockSpec((None, ts, gd), lambda b, i, g: (b, i, g)),
        out_shape=jax.ShapeDtypeStruct((bsz, s, n_g * gd), BF16),
        compiler_params=_params(("parallel", "parallel", "parallel")),
        name="pool_mix",
    )(proj, proj, pool_w, pool_scale.reshape(1, n_g * gd))


def _for_chunks(n, fn):
    def pair(jj, carry):
        fn(2 * jj)
        fn(2 * jj + 1)
        return carry

    lax.fori_loop(0, n // 2, pair, 0)

    @pl.when(n % 2 == 1)
    def _():
        fn(n - 1)


def _attn_body(lam_ref, g_ref, q_ref, k_ref, v_ref, o_ref, s_ref, m_ref, l_ref, acc_ref, *, tq, lambda_init):
    dh = DA_HEAD_DIM
    i = pl.program_id(2)
    maps = range(2)
    slabs = [slice(b * LANES, (b + 1) * LANES) for b in range(tq // LANES)]
    q = (q_ref[...].astype(F32) * (dh ** -0.5 * LOG2E)).astype(BF16)

    def rows(j):
        return pl.ds(pl.multiple_of(j * tq, tq), tq)

    def scores(j, c):
        cols = slice(c * dh, (c + 1) * dh)
        return lax.dot_general(q[:, cols], k_ref[rows(j), cols], (((1,), (1,)), ((), ())),
                               preferred_element_type=F32)

    def keep(j, c, s):
        s_ref[c, j] = s
        m = m_ref[c]
        for sl in slabs:
            m = jnp.maximum(m, s[:, sl])
        m_ref[c] = m

    def pass1(j):
        for c in maps:
            keep(j, c, scores(j, c))

    m_ref[...] = jnp.full_like(m_ref, -jnp.inf)
    _for_chunks(i, pass1)
    row = lax.broadcasted_iota(jnp.int32, (tq, tq), 0)
    col = lax.broadcasted_iota(jnp.int32, (tq, tq), 1)
    for c in maps:
        keep(i, c, jnp.where(col <= row, scores(i, c), -jnp.inf))
    mb = [jnp.broadcast_to(jnp.max(m_ref[c], axis=-1, keepdims=True), (tq, LANES)) for c in maps]

    def pass2(j):
        p_maps = []
        for c in maps:
            s = s_ref[c, j]
            l = l_ref[c]
            ps = []
            for sl in slabs:
                p = jnp.exp2(s[:, sl] - mb[c])
                l = l + p
                ps.append(p.astype(BF16))
            l_ref[c] = l
            p_maps.append(jnp.concatenate(ps, axis=1))
        acc_ref[...] += jnp.dot(jnp.concatenate(p_maps, axis=0), v_ref[rows(j), :], preferred_element_type=F32)

    l_ref[...] = jnp.zeros_like(l_ref)
    acc_ref[...] = jnp.zeros_like(acc_ref)
    _for_chunks(i + 1, pass2)
    outs = [acc_ref[c * tq:(c + 1) * tq] / jnp.sum(l_ref[c], axis=-1, keepdims=True) for c in maps]

    lp = lam_ref[...]
    lam = (jnp.exp(jnp.sum(lp[0:1] * lp[1:2], axis=-1, keepdims=True))
           - jnp.exp(jnp.sum(lp[2:3] * lp[3:4], axis=-1, keepdims=True)) + lambda_init)
    o = outs[0] - lam * outs[1]
    o = o * lax.rsqrt(jnp.mean(o * o, axis=-1, keepdims=True) + SUBLN_EPS)
    o_ref[...] = (o * g_ref[...] * (1.0 - lambda_init)).astype(o_ref.dtype)


def diff_attention(proj, col, lam, subln_g, lambda_init, tq=512):
    bsz, s, _ = proj.shape
    hw = 2 * DA_HEAD_DIM
    assert col % hw == 0
    tq = _tile(s, tq)
    qb = col // hw
    kb, vb = qb + DA_HEADS, qb + 2 * DA_HEADS
    return pl.pallas_call(
        functools.partial(_attn_body, tq=tq, lambda_init=lambda_init),
        grid=(bsz, DA_HEADS, s // tq),
        in_specs=[pl.BlockSpec((4, DA_HEAD_DIM), lambda b, h, i: (0, 0)),
                  pl.BlockSpec((1, hw), lambda b, h, i: (0, 0)),
                  pl.BlockSpec((None, tq, hw), lambda b, h, i: (b, i, qb + h)),
                  pl.BlockSpec((None, s, hw), lambda b, h, i: (b, 0, kb + h)),
                  pl.BlockSpec((None, s, hw), lambda b, h, i: (b, 0, vb + h))],
        out_specs=pl.BlockSpec((None, tq, hw), lambda b, h, i: (b, i, h)),
        out_shape=jax.ShapeDtypeStruct((bsz, s, DA_HEADS * hw), BF16),
        scratch_shapes=[pltpu.VMEM((2, s // tq, tq, tq), F32), pltpu.VMEM((2, tq, LANES), F32),
                        pltpu.VMEM((2, tq, LANES), F32), pltpu.VMEM((2 * tq, hw), F32)],
        compiler_params=_params(("parallel", "parallel", "arbitrary")),
        name="diff_attention",
    )(lam, subln_g.reshape(1, hw), proj, proj, proj)


MOE_ROWS = 512
MOE_TOKENS = 256


def _route(ei, n_e):
    n = ei.shape[0]
    e = ei[:, :TOP_K]
    onehot = (e[:, :, None] == jnp.arange(n_e, dtype=jnp.int32)).astype(jnp.int32).reshape(n * TOP_K, n_e)
    incl = jnp.cumsum(onehot, axis=0)
    rank = jnp.sum((incl - onehot) * onehot, axis=1)
    tiles = (incl[-1] + MOE_ROWS - 1) // MOE_ROWS
    tile_end = jnp.cumsum(tiles)
    start = (tile_end - tiles) * MOE_ROWS
    dest = (jnp.sum(onehot * start[None, :], axis=1) + rank).reshape(n, TOP_K)
    max_tiles = (n * TOP_K) // MOE_ROWS + n_e
    n_used = tile_end[-1:]
    t = jnp.minimum(jnp.arange(max_tiles, dtype=jnp.int32), n_used[0] - 1)
    tile_expert = jnp.sum((t[:, None] >= tile_end[None, :]).astype(jnp.int32), axis=1)
    return dest.astype(jnp.int32), tile_expert.astype(jnp.int32), n_used.astype(jnp.int32), max_tiles


def _dispatch_body(dest_ref, hp_ref, init_ref, hs_ref, sem, *, tt):
    del init_ref

    def row_copy(t, k):
        return pltpu.make_async_copy(hp_ref.at[pl.ds(t, 1)], hs_ref.at[pl.ds(dest_ref[0, TOP_K * t + k], 1)], sem)

    def start(t, carry):
        for k in range(TOP_K):
            row_copy(t, k).start()
        return carry

    def wait(t, carry):
        for k in range(TOP_K):
            row_copy(t, k).wait()
        return carry

    lax.fori_loop(0, tt, start, 0)
    lax.fori_loop(0, tt, wait, 0)


def moe_dispatch(hp, dest, n_rows):
    n, w = hp.shape
    tt = _tile(n, MOE_TOKENS, 8)
    return pl.pallas_call(
        functools.partial(_dispatch_body, tt=tt),
        grid=(n // tt,),
        in_specs=[pl.BlockSpec((None, 1, TOP_K * tt), lambda i: (i, 0, 0), memory_space=pltpu.SMEM),
                  pl.BlockSpec((tt, w), lambda i: (i, 0)),
                  pl.BlockSpec(memory_space=pl.ANY)],
        out_specs=pl.BlockSpec(memory_space=pl.ANY),
        out_shape=jax.ShapeDtypeStruct((n_rows, w), U32),
        scratch_shapes=[pltpu.SemaphoreType.DMA(())],
        input_output_aliases={2: 0},
        compiler_params=_params(("arbitrary",)),
        name="moe_dispatch",
    )(dest.reshape(n // tt, 1, TOP_K * tt), hp, jnp.zeros((n_rows, w), U32))


def _zero_unused_tile(nu_ref, o_ref):
    @pl.when(pl.program_id(1) >= nu_ref[0])
    def _():
        o_ref[...] = jnp.zeros_like(o_ref)


def _moe_glu_body(te_ref, nu_ref, a_ref, wg_ref, wu_ref, o_ref, wgb_ref, wub_ref):
    r = pl.program_id(1)
    _zero_unused_tile(nu_ref, o_ref)

    @pl.when((r == 0) | (te_ref[r] != te_ref[jnp.maximum(r - 1, 0)]))
    def _():
        wgb_ref[...] = wg_ref[...].astype(BF16)
        wub_ref[...] = wu_ref[...].astype(BF16)

    @pl.when(r < nu_ref[0])
    def _():
        lo, hi = _unpack_halves(a_ref[...])
        half = lo.shape[1]

        def proj(w_ref):
            return (jnp.dot(lo, w_ref[:half], preferred_element_type=F32)
                    + jnp.dot(hi, w_ref[half:], preferred_element_type=F32))

        o_ref[...] = _silu_mul(proj(wgb_ref), proj(wub_ref)).astype(o_ref.dtype)


def moe_glu(hs, wg, wu, layer, tile_expert, n_used, tn=512):
    rows, half = hs.shape
    f = wg.shape[3]
    tn = _tile(f, tn, MXU_EDGE)
    n_tiles = rows // MOE_ROWS

    def row_tile(r, nu):
        return jnp.minimum(r, nu[0] - 1)

    w_spec = pl.BlockSpec((None, None, 2 * half, tn), lambda j, r, te, nu: (layer, te[r], 0, j))
    return pl.pallas_call(
        _moe_glu_body,
        grid_spec=pltpu.PrefetchScalarGridSpec(
            num_scalar_prefetch=2,
            grid=(f // tn, n_tiles),
            in_specs=[pl.BlockSpec((MOE_ROWS, half), lambda j, r, te, nu: (row_tile(r, nu), 0)), w_spec, w_spec],
            out_specs=pl.BlockSpec((MOE_ROWS, tn), lambda j, r, te, nu: (r, j)),
            scratch_shapes=[pltpu.VMEM((2 * half, tn), BF16), pltpu.VMEM((2 * half, tn), BF16)]),
        out_shape=jax.ShapeDtypeStruct((rows, f), BF16),
        compiler_params=_params(("arbitrary", "arbitrary")),
        name="moe_glu",
    )(tile_expert, n_used, hs, wg, wu)


def _moe_down_body(te_ref, nu_ref, a_ref, w_ref, o_ref):
    del te_ref
    _zero_unused_tile(nu_ref, o_ref)

    @pl.when(pl.program_id(1) < nu_ref[0])
    def _():
        o_ref[...] = jnp.dot(a_ref[...], w_ref[...].astype(BF16), preferred_element_type=F32)


def moe_down(act, wd, layer, tile_expert, n_used, tn=1024):
    rows, f = act.shape
    d = wd.shape[3]
    tn = _tile(d, tn)
    n_tiles = rows // MOE_ROWS

    def row_tile(r, nu):
        return jnp.minimum(r, nu[0] - 1)

    return pl.pallas_call(
        _moe_down_body,
        grid_spec=pltpu.PrefetchScalarGridSpec(
            num_scalar_prefetch=2,
            grid=(d // tn, n_tiles),
            in_specs=[pl.BlockSpec((MOE_ROWS, f), lambda j, r, te, nu: (row_tile(r, nu), 0)),
                      pl.BlockSpec((None, None, f, tn), lambda j, r, te, nu: (layer, te[r], 0, j))],
            out_specs=pl.BlockSpec((MOE_ROWS, tn), lambda j, r, te, nu: (r, j))),
        out_shape=jax.ShapeDtypeStruct((rows, d), F32),
        compiler_params=_params(("arbitrary", "arbitrary")),
        name="moe_down",
    )(tile_expert, n_used, act, wd)


def _combine_body(dest_ref, next_ref, ew_ref, x_ref, *rest, tt, has_norm):
    if has_norm:
        g_ref, ys_ref, o_ref, ybuf, sem = rest
    else:
        ys_ref, o_ref, ybuf, sem = rest
    i, n_steps = pl.program_id(0), pl.num_programs(0)
    slot = i % 2

    def row_copy(d_ref, sl, t, k):
        return pltpu.make_async_copy(ys_ref.at[pl.ds(d_ref[0, TOP_K * t + k], 1)],
                                     ybuf.at[sl, k, pl.ds(t, 1)], sem.at[sl])

    def start_all(d_ref, sl):
        def start(t, carry):
            for k in range(TOP_K):
                row_copy(d_ref, sl, t, k).start()
            return carry

        lax.fori_loop(0, tt, start, 0)

    @pl.when(i == 0)
    def _():
        start_all(dest_ref, slot)

    @pl.when(i + 1 < n_steps)
    def _():
        start_all(next_ref, 1 - slot)

    def wait(t, carry):
        for k in range(TOP_K):
            row_copy(dest_ref, slot, t, k).wait()
        return carry

    lax.fori_loop(0, tt, wait, 0)
    ew = ew_ref[...]
    y = x_ref[...] + ew[:, 0:1] * ybuf[slot, 0] + ew[:, 1:2] * ybuf[slot, 1]
    o_ref[...] = _rms(y, g_ref[...]) if has_norm else y


def moe_combine(x, ys, dest, ew, norm_g=None):
    n, d = x.shape
    tt = _tile(n, MOE_TOKENS, 8)
    n_steps = n // tt
    dest = dest.reshape(n_steps, 1, TOP_K * tt)

    def d_spec(ahead):
        return pl.BlockSpec((None, 1, TOP_K * tt), lambda i: (jnp.minimum(i + ahead, n_steps - 1), 0, 0),
                            memory_space=pltpu.SMEM)

    in_specs = [d_spec(0), d_spec(1),
                pl.BlockSpec((tt, LANES), lambda i: (i, 0)),
                pl.BlockSpec((tt, d), lambda i: (i, 0))]
    args = [dest, dest, ew, x]
    if norm_g is not None:
        in_specs.append(pl.BlockSpec((1, d), lambda i: (0, 0)))
        args.append(norm_g.reshape(1, d))
    return pl.pallas_call(
        functools.partial(_combine_body, tt=tt, has_norm=norm_g is not None),
        grid=(n_steps,),
        in_specs=in_specs + [pl.BlockSpec(memory_space=pl.ANY)],
        out_specs=pl.BlockSpec((tt, d), lambda i: (i, 0)),
        out_shape=jax.ShapeDtypeStruct((n, d), F32),
        scratch_shapes=[pltpu.VMEM((2, TOP_K, tt, d), F32), pltpu.SemaphoreType.DMA((2,))],
        compiler_params=_params(("arbitrary",)),
        name="moe_combine",
    )(*args, ys)


def moe_ffn(x, g, w_router, wg, wu, wd, layer, norm_g=None):
    n_e = wg.shape[1]
    hp, ei, ew = rmsnorm_router(x, g, w_router)
    dest, tile_expert, n_used, max_tiles = _route(ei, n_e)
    hs = moe_dispatch(hp, dest, max_tiles * MOE_ROWS)
    act = moe_glu(hs, wg, wu, layer, tile_expert, n_used)
    ys = moe_down(act, wd, layer, tile_expert, n_used)
    return moe_combine(x, ys, dest, ew, norm_g)


def _mixer(x, h, l, w_in, conv_w, w_a_out, pool_w, pool_scale, w_p_out, lam, subln, w_c_out, w_o,
           lambda_init, bsz, s):
    n, d = x.shape
    cw, pw, aw = conv_w.shape[2], pool_scale.shape[1], w_c_out.shape[1]
    proj = matmul(h, w_in, l, BF16)
    proj3 = proj.reshape(bsz, s, proj.shape[1])
    ya = short_conv(proj3, conv_w[l]).reshape(n, cw)
    yp = pool_mix(proj3, 3 * cw, pool_w[l], pool_scale[l]).reshape(n, pw)
    yc = diff_attention(proj3, 3 * cw + pw, lam[l], subln[l], lambda_init).reshape(n, aw)
    merged = merge(ya, yp, yc, w_a_out, w_p_out, w_c_out, l, proj, 3 * cw + pw + 3 * aw)
    return matmul(merged, w_o, l, F32, res=x, tn=256)


def kernel(x, norm_mix, w_in, conv_w, w_a_out, pool_w, pool_scale, w_p_out, lam, subln, w_c_out, w_o,
           norm_ffn, ffn_w_gate, ffn_w_up, ffn_w_down, w_router, moe_w_gate, moe_w_up, moe_w_down,
           norm_final):
    bsz, s, d = x.shape
    depth = norm_mix.shape[0]
    n = bsz * s
    x = x.reshape(n, d)
    for l in range(depth):
        lambda_init = 0.8 - 0.6 * math.exp(-0.3 * l)
        h = rmsnorm(x, norm_mix[l], BF16)
        x = _mixer(x, h, l, w_in, conv_w, w_a_out, pool_w, pool_scale, w_p_out, lam, subln, w_c_out, w_o,
                   lambda_init, bsz, s)
        j = l // 2
        if l % 2 == 0:
            h = rmsnorm(x, norm_ffn[l], BF16)
            act = glu_in(h, ffn_w_gate, ffn_w_up, j)
            x = matmul_acc(act, ffn_w_down, j, x)
        elif l < depth - 1:
            x = moe_ffn(x, norm_ffn[l], w_router[j], moe_w_gate, moe_w_up, moe_w_down, j)
        else:
            return moe_ffn(x, norm_ffn[l], w_router[j], moe_w_gate, moe_w_up, moe_w_down, j,
                           norm_g=norm_final).reshape(bsz, s, d)
    return rmsnorm(x, norm_final, F32).reshape(bsz, s, d)
```

```python
import functools
import math

import jax
import jax.numpy as jnp
from jax import lax
from jax.experimental import pallas as pl
from jax.experimental.pallas import tpu as pltpu

F32 = jnp.float32
BF16 = jnp.bfloat16
U32 = jnp.uint32

NORM_EPS = 1e-6
SUBLN_EPS = 1e-5
CONV_K = 3
POOL_WINDOWS = (2, 4, 8, 16)
DA_HEADS = 8
DA_HEAD_DIM = 128
N_BRANCHES = 3
TOP_K = 2

LANES = 128
BF16_ROWS = 16
MXU_EDGE = 256
VMEM_BYTES = 64 * 1024 * 1024
VMEM_LIMIT = VMEM_BYTES - 8 * 1024 * 1024

HALO = BF16_ROWS
assert HALO >= max(POOL_WINDOWS) and HALO >= CONV_K
LOG2E = math.log2(math.e)
HI16 = 0xFFFF0000


def _tile(dim, pref, unit=LANES):
    if dim <= pref:
        return dim
    t = (pref // unit) * unit
    while t > unit and dim % t:
        t -= unit
    assert dim % t == 0, (dim, pref, unit)
    return t


def _params(sem):
    return pltpu.CompilerParams(dimension_semantics=sem, vmem_limit_bytes=VMEM_LIMIT)


def _resident(block_shape, index_map):
    return pl.BlockSpec(block_shape, index_map, pipeline_mode=pl.Buffered(1))


def _rms(x, g):
    ms = jnp.mean(x * x, axis=-1, keepdims=True)
    return x * lax.rsqrt(ms + NORM_EPS) * g


def _rmsnorm_body(x_ref, g_ref, o_ref):
    o_ref[...] = _rms(x_ref[...], g_ref[...]).astype(o_ref.dtype)


def rmsnorm(x, g, out_dtype):
    n, d = x.shape
    tr = _tile(n, 256, 8)
    return pl.pallas_call(
        _rmsnorm_body,
        grid=(n // tr,),
        in_specs=[pl.BlockSpec((tr, d), lambda i: (i, 0)),
                  pl.BlockSpec((1, d), lambda i: (0, 0))],
        out_specs=pl.BlockSpec((tr, d), lambda i: (i, 0)),
        out_shape=jax.ShapeDtypeStruct((n, d), out_dtype),
        compiler_params=_params(("parallel",)),
        name="rmsnorm",
    )(x, g.reshape(1, d))


def _pack_halves(h):
    half = h.shape[1] // 2
    lo = lax.bitcast_convert_type(h[:, :half].astype(BF16).astype(F32), U32)
    hi = lax.bitcast_convert_type(h[:, half:].astype(BF16).astype(F32), U32)
    return (lo >> 16) | (hi & jnp.uint32(HI16))


def _unpack_halves(p):
    lo = lax.bitcast_convert_type(p << 16, F32).astype(BF16)
    hi = lax.bitcast_convert_type(p & jnp.uint32(HI16), F32).astype(BF16)
    return lo, hi


def _split_bf16(x):
    hi = x.astype(BF16)
    return hi, (x - hi.astype(F32)).astype(BF16)


def _rmsnorm_router_body(x_ref, g_ref, wr_ref, hp_ref, ei_ref, ew_ref, *, n_experts):
    h = _rms(x_ref[...], g_ref[...])
    hp_ref[...] = _pack_halves(h)
    h_hi, h_lo = _split_bf16(h)
    w_hi, w_lo = _split_bf16(wr_ref[...])
    logits = (jnp.dot(h_hi, w_hi, preferred_element_type=F32) + jnp.dot(h_hi, w_lo, preferred_element_type=F32)
              + jnp.dot(h_lo, w_hi, preferred_element_type=F32))
    lane = lax.broadcasted_iota(jnp.int32, logits.shape, 1)
    neg = jnp.float32(-jnp.inf)
    logits = jnp.where(lane < n_experts, logits, neg)
    m1 = jnp.max(logits, axis=-1, keepdims=True)
    i1 = jnp.min(jnp.where(logits == m1, lane, LANES), axis=-1, keepdims=True)
    rest = jnp.where(lane == i1, neg, logits)
    m2 = jnp.max(rest, axis=-1, keepdims=True)
    i2 = jnp.min(jnp.where(rest == m2, lane, LANES), axis=-1, keepdims=True)
    e2 = jnp.exp(m2 - m1)
    w1 = 1.0 / (1.0 + e2)
    w2 = e2 / (1.0 + e2)
    ei_ref[...] = jnp.where(lane == 0, i1, jnp.where(lane == 1, i2, 0))
    ew_ref[...] = jnp.where(lane == 0, w1, jnp.where(lane == 1, w2, 0.0))


def rmsnorm_router(x, g, w_router):
    n, d = x.shape
    n_experts = w_router.shape[1]
    wr = jnp.pad(w_router, ((0, 0), (0, LANES - n_experts)))
    tr = _tile(n, 256, 8)
    return pl.pallas_call(
        functools.partial(_rmsnorm_router_body, n_experts=n_experts),
        grid=(n // tr,),
        in_specs=[pl.BlockSpec((tr, d), lambda i: (i, 0)),
                  pl.BlockSpec((1, d), lambda i: (0, 0)),
                  pl.BlockSpec((d, LANES), lambda i: (0, 0))],
        out_specs=[pl.BlockSpec((tr, d // 2), lambda i: (i, 0)),
                   pl.BlockSpec((tr, LANES), lambda i: (i, 0)),
                   pl.BlockSpec((tr, LANES), lambda i: (i, 0))],
        out_shape=[jax.ShapeDtypeStruct((n, d // 2), U32),
                   jax.ShapeDtypeStruct((n, LANES), jnp.int32),
                   jax.ShapeDtypeStruct((n, LANES), F32)],
        compiler_params=_params(("parallel",)),
        name="rmsnorm_router",
    )(x, g.reshape(1, d), wr)


def _mm_body(a_ref, w_ref, *rest, has_res):
    r = jnp.dot(a_ref[...], w_ref[...].astype(BF16), preferred_element_type=F32)
    if has_res:
        r = r + rest[0][...]
    rest[-1][...] = r.astype(rest[-1].dtype)


def matmul(a, w, layer, out_dtype, res=None, tm=2048, tn=512):
    m, kd = a.shape
    n = w.shape[2]
    tm, tn = _tile(m, tm), _tile(n, tn)
    in_specs = [_resident((tm, kd), lambda i, j: (i, 0)),
                pl.BlockSpec((None, kd, tn), lambda i, j: (layer, 0, j))]
    args = [a, w]
    if res is not None:
        in_specs.append(pl.BlockSpec((tm, tn), lambda i, j: (i, j)))
        args.append(res)
    return pl.pallas_call(
        functools.partial(_mm_body, has_res=res is not None),
        grid=(m // tm, n // tn),
        in_specs=in_specs,
        out_specs=pl.BlockSpec((tm, tn), lambda i, j: (i, j)),
        out_shape=jax.ShapeDtypeStruct((m, n), out_dtype),
        compiler_params=_params(("parallel", "arbitrary")),
        name="matmul_res" if res is not None else "matmul",
    )(*args)


def _mm_acc_body(a_ref, w_ref, res_ref, o_ref):
    @pl.when(pl.program_id(2) == 0)
    def _():
        o_ref[...] = res_ref[...]

    o_ref[...] += jnp.dot(a_ref[...], w_ref[...].astype(BF16), preferred_element_type=F32)


def matmul_acc(a, w, layer, res, tm=2048, tn=512, tk=1792):
    m, kd = a.shape
    n = w.shape[2]
    tm, tn, tk = _tile(m, tm), _tile(n, tn, MXU_EDGE), _tile(kd, tk, MXU_EDGE)
    return pl.pallas_call(
        _mm_acc_body,
        grid=(m // tm, n // tn, kd // tk),
        in_specs=[pl.BlockSpec((tm, tk), lambda i, j, k: (i, k)),
                  pl.BlockSpec((None, tk, tn), lambda i, j, k: (layer, k, j)),
                  pl.BlockSpec((tm, tn), lambda i, j, k: (i, j))],
        out_specs=pl.BlockSpec((tm, tn), lambda i, j, k: (i, j)),
        out_shape=jax.ShapeDtypeStruct((m, n), F32),
        compiler_params=_params(("parallel", "parallel", "arbitrary")),
        name="matmul_acc",
    )(a, w, res)


def _silu_mul(g, u):
    return g * jax.nn.sigmoid(g) * u


def _glu_body(a_ref, wg_ref, wu_ref, o_ref):
    a = a_ref[...]
    g = jnp.dot(a, wg_ref[...].astype(BF16), preferred_element_type=F32)
    u = jnp.dot(a, wu_ref[...].astype(BF16), preferred_element_type=F32)
    o_ref[...] = _silu_mul(g, u).astype(o_ref.dtype)


def glu_in(a, wg, wu, layer, tm=2048, tn=256):
    m, kd = a.shape
    f = wg.shape[2]
    tm, tn = _tile(m, tm), _tile(f, tn, MXU_EDGE)
    w_spec = pl.BlockSpec((None, kd, tn), lambda i, j: (layer, 0, j))
    return pl.pallas_call(
        _glu_body,
        grid=(m // tm, f // tn),
        in_specs=[pl.BlockSpec((tm, kd), lambda i, j: (i, 0)), w_spec, w_spec],
        out_specs=pl.BlockSpec((tm, tn), lambda i, j: (i, j)),
        out_shape=jax.ShapeDtypeStruct((m, f), BF16),
        compiler_params=_params(("parallel", "arbitrary")),
        name="glu_in",
    )(a, wg, wu)


MERGE_ROWS = 256


def _merge_body(a_ref, p_ref, c_ref, wa_ref, wp_ref, wc_ref, g0_ref, g1_ref, g2_ref, o_ref):
    branches = ((a_ref, wa_ref[...].astype(BF16), g0_ref),
                (p_ref, wp_ref[...].astype(BF16), g1_ref),
                (c_ref, wc_ref[...].astype(BF16), g2_ref))
    rc = min(MERGE_ROWS, o_ref.shape[0])
    for r in range(o_ref.shape[0] // rc):
        rows = slice(r * rc, (r + 1) * rc)
        acc = None
        for x_ref, w, g_ref in branches:
            y = jax.nn.sigmoid(g_ref[rows].astype(F32)) * jnp.dot(x_ref[rows], w, preferred_element_type=F32)
            acc = y if acc is None else acc + y
        o_ref[rows] = acc.astype(o_ref.dtype)


def merge(ya, yp, yc, wa, wp, wc, layer, proj, gate_col, tm=2048, tn=256):
    m = ya.shape[0]
    d = wa.shape[2]
    tm, tn = _tile(m, tm), _tile(d, tn)
    assert gate_col % tn == 0
    gb, nd = gate_col // tn, d // tn

    def x_spec(x):
        return pl.BlockSpec((tm, x.shape[1]), lambda i, j: (i, 0))

    def w_spec(w):
        return pl.BlockSpec((None, w.shape[1], tn), lambda i, j: (layer, 0, j))

    def g_spec(b):
        return pl.BlockSpec((tm, tn), lambda i, j: (i, gb + b * nd + j))

    return pl.pallas_call(
        _merge_body,
        grid=(m // tm, nd),
        in_specs=[x_spec(ya), x_spec(yp), x_spec(yc), w_spec(wa), w_spec(wp), w_spec(wc),
                  g_spec(0), g_spec(1), g_spec(2)],
        out_specs=pl.BlockSpec((tm, tn), lambda i, j: (i, j)),
        out_shape=jax.ShapeDtypeStruct((m, d), BF16),
        compiler_params=_params(("parallel", "parallel")),
        name="merge",
    )(ya, yp, yc, wa, wp, wc, proj, proj, proj)


def _with_halo(prev, cur, first):
    prev = jnp.where(first, 0.0, prev)
    return jnp.concatenate([prev, cur], axis=0)


def _conv_body(u_ref, b_ref, c_ref, up_ref, cp_ref, w_ref, o_ref):
    first = pl.program_id(1) == 0
    z = c_ref[...].astype(F32) * u_ref[...].astype(F32)
    zp = cp_ref[...].astype(F32) * up_ref[...].astype(F32)
    ze = _with_halo(zp, z, first)
    z1 = pltpu.roll(ze, 1, 0)[HALO:]
    z2 = pltpu.roll(ze, 2, 0)[HALO:]
    w = w_ref[...]
    zc = w[0:1] * z2 + w[1:2] * z1 + w[2:3] * z
    o_ref[...] = (b_ref[...].astype(F32) * zc).astype(o_ref.dtype)


def short_conv(proj, conv_w, ts=2048, cw=256):
    bsz, s, _ = proj.shape
    width = conv_w.shape[1]
    ts, cw = _tile(s, ts, HALO), _tile(width, cw)
    nc, hb = width // cw, ts // HALO

    def cur(off):
        return pl.BlockSpec((None, ts, cw), lambda b, i, j: (b, i, off * nc + j))

    def prev(off):
        return pl.BlockSpec((None, HALO, cw), lambda b, i, j: (b, jnp.maximum(i * hb - 1, 0), off * nc + j))

    return pl.pallas_call(
        _conv_body,
        grid=(bsz, s // ts, nc),
        in_specs=[cur(0), cur(1), cur(2), prev(0), prev(2),
                  pl.BlockSpec((CONV_K, cw), lambda b, i, j: (0, j))],
        out_specs=pl.BlockSpec((None, ts, cw), lambda b, i, j: (b, i, j)),
        out_shape=jax.ShapeDtypeStruct((bsz, s, width), BF16),
        compiler_params=_params(("parallel", "parallel", "parallel")),
        name="short_conv",
    )(proj, proj, proj, proj, proj, conv_w)


def _pool_body(u_ref, up_ref, pw_ref, ps_ref, o_ref, *, ts):
    i, g = pl.program_id(1), pl.program_id(2)
    u = u_ref[...].astype(F32)
    ue = _with_halo(up_ref[...].astype(F32), u, i == 0)
    t1 = i * ts + lax.broadcasted_iota(jnp.int32, u.shape, 0) + 1
    for gi, w in enumerate(POOL_WINDOWS):
        assert w & (w - 1) == 0 and w <= HALO

        @pl.when(g == gi)
        def _(w=w):
            s, span = ue, 1
            while span < w:
                s = s + pltpu.roll(s, span, 0)
                span *= 2
            mean = s[HALO:] / jnp.minimum(t1, w).astype(F32)
            mixed = (mean - u).astype(BF16)
            y = jnp.dot(mixed, pw_ref[...].astype(BF16), preferred_element_type=F32)
            o_ref[...] = (y * ps_ref[...]).astype(o_ref.dtype)


def pool_mix(proj, col, pool_w, pool_scale, ts=2048):
    bsz, s, _ = proj.shape
    n_g, gd, _ = pool_w.shape
    assert n_g == len(POOL_WINDOWS) and col % gd == 0
    ts = _tile(s, ts, HALO)
    cb, hb = col // gd, ts // HALO
    return pl.pallas_call(
        functools.partial(_pool_body, ts=ts),
        grid=(bsz, s // ts, n_g),
        in_specs=[pl.BlockSpec((None, ts, gd), lambda b, i, g: (b, i, cb + g)),
                  pl.BlockSpec((None, HALO, gd), lambda b, i, g: (b, jnp.maximum(i * hb - 1, 0), cb + g)),
                  pl.BlockSpec((None, gd, gd), lambda b, i, g: (g, 0, 0)),
                  pl.BlockSpec((1, gd), lambda b, i, g: (0, g))],
        out_specs=pl.BlockSpec((None, ts, gd), lambda b, i, g: (b, i, g)),
        out_shape=jax.ShapeDtypeStruct((bsz, s, n_g * gd), BF16),
        compiler_params=_params(("parallel", "parallel", "parallel")),
        name="pool_mix",
    )(proj, proj, pool_w, pool_scale.reshape(1, n_g * gd))


CHUNK_GROUP = 2


def _for_chunks(n, fn):
    def group(jj, carry):
        for u in range(CHUNK_GROUP):
            fn(CHUNK_GROUP * jj + u)
        return carry

    def single(j, carry):
        fn(j)
        return carry

    lax.fori_loop(0, n // CHUNK_GROUP, group, 0)
    lax.fori_loop(n - n % CHUNK_GROUP, n, single, 0)


def _attn_body(lam_ref, g_ref, q_ref, k_ref, v_ref, o_ref, s_ref, m_ref, l_ref, acc_ref, *, tq, lambda_init):
    dh = DA_HEAD_DIM
    i = pl.program_id(2)
    maps = range(2)
    slabs = [slice(b * LANES, (b + 1) * LANES) for b in range(tq // LANES)]
    q = (q_ref[...].astype(F32) * (dh ** -0.5 * LOG2E)).astype(BF16)

    def rows(j):
        return pl.ds(pl.multiple_of(j * tq, tq), tq)

    def scores(j, c):
        cols = slice(c * dh, (c + 1) * dh)
        return lax.dot_general(q[:, cols], k_ref[rows(j), cols], (((1,), (1,)), ((), ())),
                               preferred_element_type=F32)

    def keep(j, c, s):
        s_ref[c, j] = s
        m = m_ref[c]
        for sl in slabs:
            m = jnp.maximum(m, s[:, sl])
        m_ref[c] = m

    def pass1(j):
        for c in maps:
            keep(j, c, scores(j, c))

    m_ref[...] = jnp.full_like(m_ref, -jnp.inf)
    _for_chunks(i, pass1)
    row = lax.broadcasted_iota(jnp.int32, (tq, tq), 0)
    col = lax.broadcasted_iota(jnp.int32, (tq, tq), 1)
    for c in maps:
        keep(i, c, jnp.where(col <= row, scores(i, c), -jnp.inf))
    mb = [jnp.broadcast_to(jnp.max(m_ref[c], axis=-1, keepdims=True), (tq, LANES)) for c in maps]

    def pass2(j):
        p_maps = []
        for c in maps:
            s = s_ref[c, j]
            l = l_ref[c]
            ps = []
            for sl in slabs:
                p = jnp.exp2(s[:, sl] - mb[c])
                l = l + p
                ps.append(p.astype(BF16))
            l_ref[c] = l
            p_maps.append(jnp.concatenate(ps, axis=1))
        acc_ref[...] += jnp.dot(jnp.concatenate(p_maps, axis=0), v_ref[rows(j), :], preferred_element_type=F32)

    l_ref[...] = jnp.zeros_like(l_ref)
    acc_ref[...] = jnp.zeros_like(acc_ref)
    _for_chunks(i + 1, pass2)
    outs = [acc_ref[c * tq:(c + 1) * tq] / jnp.sum(l_ref[c], axis=-1, keepdims=True) for c in maps]

    lp = lam_ref[...]
    lam = (jnp.exp(jnp.sum(lp[0:1] * lp[1:2], axis=-1, keepdims=True))
           - jnp.exp(jnp.sum(lp[2:3] * lp[3:4], axis=-1, keepdims=True)) + lambda_init)
    o = outs[0] - lam * outs[1]
    o = o * lax.rsqrt(jnp.mean(o * o, axis=-1, keepdims=True) + SUBLN_EPS)
    o_ref[...] = (o * g_ref[...] * (1.0 - lambda_init)).astype(o_ref.dtype)


def diff_attention(proj, col, lam, subln_g, lambda_init, tq=512):
    bsz, s, _ = proj.shape
    hw = 2 * DA_HEAD_DIM
    assert col % hw == 0
    tq = _tile(s, tq)
    qb = col // hw
    kb, vb = qb + DA_HEADS, qb + 2 * DA_HEADS
    return pl.pallas_call(
        functools.partial(_attn_body, tq=tq, lambda_init=lambda_init),
        grid=(bsz, DA_HEADS, s // tq),
        in_specs=[pl.BlockSpec((4, DA_HEAD_DIM), lambda b, h, i: (0, 0)),
                  pl.BlockSpec((1, hw), lambda b, h, i: (0, 0)),
                  pl.BlockSpec((None, tq, hw), lambda b, h, i: (b, i, qb + h)),
                  pl.BlockSpec((None, s, hw), lambda b, h, i: (b, 0, kb + h)),
                  pl.BlockSpec((None, s, hw), lambda b, h, i: (b, 0, vb + h))],
        out_specs=pl.BlockSpec((None, tq, hw), lambda b, h, i: (b, i, h)),
        out_shape=jax.ShapeDtypeStruct((bsz, s, DA_HEADS * hw), BF16),
        scratch_shapes=[pltpu.VMEM((2, s // tq, tq, tq), F32), pltpu.VMEM((2, tq, LANES), F32),
                        pltpu.VMEM((2, tq, LANES), F32), pltpu.VMEM((2 * tq, hw), F32)],
        compiler_params=_params(("parallel", "parallel", "arbitrary")),
        name="diff_attention",
    )(lam, subln_g.reshape(1, hw), proj, proj, proj)


MOE_ROWS = 512
MOE_TOKENS = 256


def _route(ei, n_e):
    n = ei.shape[0]
    e = ei[:, :TOP_K]
    onehot = (e[:, :, None] == jnp.arange(n_e, dtype=jnp.int32)).astype(jnp.int32).reshape(n * TOP_K, n_e)
    incl = jnp.cumsum(onehot, axis=0)
    rank = jnp.sum((incl - onehot) * onehot, axis=1)
    tiles = (incl[-1] + MOE_ROWS - 1) // MOE_ROWS
    tile_end = jnp.cumsum(tiles)
    start = (tile_end - tiles) * MOE_ROWS
    dest = (jnp.sum(onehot * start[None, :], axis=1) + rank).reshape(n, TOP_K)
    max_tiles = (n * TOP_K) // MOE_ROWS + n_e
    n_used = tile_end[-1:]
    t = jnp.minimum(jnp.arange(max_tiles, dtype=jnp.int32), n_used[0] - 1)
    tile_expert = jnp.sum((t[:, None] >= tile_end[None, :]).astype(jnp.int32), axis=1)
    return dest.astype(jnp.int32), tile_expert.astype(jnp.int32), n_used.astype(jnp.int32), max_tiles


def _dispatch_body(dest_ref, hp_ref, init_ref, hs_ref, sem, *, tt):
    del init_ref

    def row_copy(t, k):
        return pltpu.make_async_copy(hp_ref.at[pl.ds(t, 1)], hs_ref.at[pl.ds(dest_ref[0, TOP_K * t + k], 1)], sem)

    def start(t, carry):
        for k in range(TOP_K):
            row_copy(t, k).start()
        return carry

    def wait(t, carry):
        for k in range(TOP_K):
            row_copy(t, k).wait()
        return carry

    lax.fori_loop(0, tt, start, 0)
    lax.fori_loop(0, tt, wait, 0)


def moe_dispatch(hp, dest, n_rows):
    n, w = hp.shape
    tt = _tile(n, MOE_TOKENS, 8)
    return pl.pallas_call(
        functools.partial(_dispatch_body, tt=tt),
        grid=(n // tt,),
        in_specs=[pl.BlockSpec((None, 1, TOP_K * tt), lambda i: (i, 0, 0), memory_space=pltpu.SMEM),
                  pl.BlockSpec((tt, w), lambda i: (i, 0)),
                  pl.BlockSpec(memory_space=pl.ANY)],
        out_specs=pl.BlockSpec(memory_space=pl.ANY),
        out_shape=jax.ShapeDtypeStruct((n_rows, w), U32),
        scratch_shapes=[pltpu.SemaphoreType.DMA(())],
        input_output_aliases={2: 0},
        compiler_params=_params(("arbitrary",)),
        name="moe_dispatch",
    )(dest.reshape(n // tt, 1, TOP_K * tt), hp, jnp.zeros((n_rows, w), U32))


def _zero_unused_tile(nu_ref, o_ref):
    @pl.when(pl.program_id(1) >= nu_ref[0])
    def _():
        o_ref[...] = jnp.zeros_like(o_ref)


def _expert_changed(te_ref, r):
    return (r == 0) | (te_ref[r] != te_ref[jnp.maximum(r - 1, 0)])


def _moe_glu_body(te_ref, nu_ref, a_ref, wg_ref, wu_ref, o_ref, wgb_ref, wub_ref):
    r = pl.program_id(1)
    _zero_unused_tile(nu_ref, o_ref)

    @pl.when(_expert_changed(te_ref, r))
    def _():
        wgb_ref[...] = wg_ref[...].astype(BF16)
        wub_ref[...] = wu_ref[...].astype(BF16)

    @pl.when(r < nu_ref[0])
    def _():
        lo, hi = _unpack_halves(a_ref[...])
        half = lo.shape[1]

        def proj(w_ref):
            return (jnp.dot(lo, w_ref[:half], preferred_element_type=F32)
                    + jnp.dot(hi, w_ref[half:], preferred_element_type=F32))

        o_ref[...] = _silu_mul(proj(wgb_ref), proj(wub_ref)).astype(o_ref.dtype)


def moe_glu(hs, wg, wu, layer, tile_expert, n_used, tn=512):
    rows, half = hs.shape
    f = wg.shape[3]
    tn = _tile(f, tn, MXU_EDGE)
    n_tiles = rows // MOE_ROWS

    def row_tile(r, nu):
        return jnp.minimum(r, nu[0] - 1)

    w_spec = pl.BlockSpec((None, None, 2 * half, tn), lambda j, r, te, nu: (layer, te[r], 0, j))
    return pl.pallas_call(
        _moe_glu_body,
        grid_spec=pltpu.PrefetchScalarGridSpec(
            num_scalar_prefetch=2,
            grid=(f // tn, n_tiles),
            in_specs=[pl.BlockSpec((MOE_ROWS, half), lambda j, r, te, nu: (row_tile(r, nu), 0)), w_spec, w_spec],
            out_specs=pl.BlockSpec((MOE_ROWS, tn), lambda j, r, te, nu: (r, j)),
            scratch_shapes=[pltpu.VMEM((2 * half, tn), BF16), pltpu.VMEM((2 * half, tn), BF16)]),
        out_shape=jax.ShapeDtypeStruct((rows, f), BF16),
        compiler_params=_params(("arbitrary", "arbitrary")),
        name="moe_glu",
    )(tile_expert, n_used, hs, wg, wu)


def _moe_down_body(te_ref, nu_ref, a_ref, w_ref, o_ref, wb_ref):
    r = pl.program_id(1)
    _zero_unused_tile(nu_ref, o_ref)

    @pl.when(_expert_changed(te_ref, r))
    def _():
        wb_ref[...] = w_ref[...].astype(BF16)

    @pl.when(r < nu_ref[0])
    def _():
        o_ref[...] = jnp.dot(a_ref[...], wb_ref[...], preferred_element_type=F32)


def moe_down(act, wd, layer, tile_expert, n_used, tn=1024):
    rows, f = act.shape
    d = wd.shape[3]
    tn = _tile(d, tn)
    n_tiles = rows // MOE_ROWS

    def row_tile(r, nu):
        return jnp.minimum(r, nu[0] - 1)

    return pl.pallas_call(
        _moe_down_body,
        grid_spec=pltpu.PrefetchScalarGridSpec(
            num_scalar_prefetch=2,
            grid=(d // tn, n_tiles),
            in_specs=[pl.BlockSpec((MOE_ROWS, f), lambda j, r, te, nu: (row_tile(r, nu), 0)),
                      pl.BlockSpec((None, None, f, tn), lambda j, r, te, nu: (layer, te[r], 0, j))],
            out_specs=pl.BlockSpec((MOE_ROWS, tn), lambda j, r, te, nu: (r, j)),
            scratch_shapes=[pltpu.VMEM((f, tn), BF16)]),
        out_shape=jax.ShapeDtypeStruct((rows, d), F32),
        compiler_params=_params(("arbitrary", "arbitrary")),
        name="moe_down",
    )(tile_expert, n_used, act, wd)


def _combine_body(dest_ref, next_ref, ew_ref, x_ref, *rest, tt, has_norm):
    if has_norm:
        g_ref, ys_ref, o_ref, ybuf, sem = rest
    else:
        ys_ref, o_ref, ybuf, sem = rest
    i, n_steps = pl.program_id(0), pl.num_programs(0)
    slot = i % 2

    def row_copy(d_ref, sl, t, k):
        return pltpu.make_async_copy(ys_ref.at[pl.ds(d_ref[0, TOP_K * t + k], 1)],
                                     ybuf.at[sl, k, pl.ds(t, 1)], sem.at[sl])

    def start_all(d_ref, sl):
        def start(t, carry):
            for k in range(TOP_K):
                row_copy(d_ref, sl, t, k).start()
            return carry

        lax.fori_loop(0, tt, start, 0)

    @pl.when(i == 0)
    def _():
        start_all(dest_ref, slot)

    @pl.when(i + 1 < n_steps)
    def _():
        start_all(next_ref, 1 - slot)

    def wait(t, carry):
        for k in range(TOP_K):
            row_copy(dest_ref, slot, t, k).wait()
        return carry

    lax.fori_loop(0, tt, wait, 0)
    ew = ew_ref[...]
    y = x_ref[...] + ew[:, 0:1] * ybuf[slot, 0] + ew[:, 1:2] * ybuf[slot, 1]
    o_ref[...] = _rms(y, g_ref[...]) if has_norm else y


def moe_combine(x, ys, dest, ew, norm_g=None):
    n, d = x.shape
    tt = _tile(n, MOE_TOKENS, 8)
    n_steps = n // tt
    dest = dest.reshape(n_steps, 1, TOP_K * tt)

    def d_spec(ahead):
        return pl.BlockSpec((None, 1, TOP_K * tt), lambda i: (jnp.minimum(i + ahead, n_steps - 1), 0, 0),
                            memory_space=pltpu.SMEM)

    in_specs = [d_spec(0), d_spec(1),
                pl.BlockSpec((tt, LANES), lambda i: (i, 0)),
                pl.BlockSpec((tt, d), lambda i: (i, 0))]
    args = [dest, dest, ew, x]
    if norm_g is not None:
        in_specs.append(pl.BlockSpec((1, d), lambda i: (0, 0)))
        args.append(norm_g.reshape(1, d))
    return pl.pallas_call(
        functools.partial(_combine_body, tt=tt, has_norm=norm_g is not None),
        grid=(n_steps,),
        in_specs=in_specs + [pl.BlockSpec(memory_space=pl.ANY)],
        out_specs=pl.BlockSpec((tt, d), lambda i: (i, 0)),
        out_shape=jax.ShapeDtypeStruct((n, d), F32),
        scratch_shapes=[pltpu.VMEM((2, TOP_K, tt, d), F32), pltpu.SemaphoreType.DMA((2,))],
        compiler_params=_params(("arbitrary",)),
        name="moe_combine",
    )(*args, ys)


def moe_ffn(x, g, w_router, wg, wu, wd, layer, norm_g=None):
    n_e = wg.shape[1]
    hp, ei, ew = rmsnorm_router(x, g, w_router)
    dest, tile_expert, n_used, max_tiles = _route(ei, n_e)
    hs = moe_dispatch(hp, dest, max_tiles * MOE_ROWS)
    act = moe_glu(hs, wg, wu, layer, tile_expert, n_used)
    ys = moe_down(act, wd, layer, tile_expert, n_used)
    return moe_combine(x, ys, dest, ew, norm_g)


def _mixer(x, h, l, w_in, conv_w, w_a_out, pool_w, pool_scale, w_p_out, lam, subln, w_c_out, w_o,
           lambda_init, bsz, s):
    n, d = x.shape
    cw, pw, aw = conv_w.shape[2], pool_scale.shape[1], w_c_out.shape[1]
    proj = matmul(h, w_in, l, BF16)
    proj3 = proj.reshape(bsz, s, proj.shape[1])
    ya = short_conv(proj3, conv_w[l]).reshape(n, cw)
    yp = pool_mix(proj3, 3 * cw, pool_w[l], pool_scale[l]).reshape(n, pw)
    yc = diff_attention(proj3, 3 * cw + pw, lam[l], subln[l], lambda_init).reshape(n, aw)
    merged = merge(ya, yp, yc, w_a_out, w_p_out, w_c_out, l, proj, 3 * cw + pw + 3 * aw)
    return matmul(merged, w_o, l, F32, res=x, tn=256)


def kernel(x, norm_mix, w_in, conv_w, w_a_out, pool_w, pool_scale, w_p_out, lam, subln, w_c_out, w_o,
           norm_ffn, ffn_w_gate, ffn_w_up, ffn_w_down, w_router, moe_w_gate, moe_w_up, moe_w_down,
           norm_final):
    bsz, s, d = x.shape
    depth = norm_mix.shape[0]
    n = bsz * s
    x = x.reshape(n, d)
    for l in range(depth):
        lambda_init = 0.8 - 0.6 * math.exp(-0.3 * l)
        h = rmsnorm(x, norm_mix[l], BF16)
        x = _mixer(x, h, l, w_in, conv_w, w_a_out, pool_w, pool_scale, w_p_out, lam, subln, w_c_out, w_o,
                   lambda_init, bsz, s)
        j = l // 2
        if l % 2 == 0:
            h = rmsnorm(x, norm_ffn[l], BF16)
            act = glu_in(h, ffn_w_gate, ffn_w_up, j)
            x = matmul_acc(act, ffn_w_down, j, x)
        elif l < depth - 1:
            x = moe_ffn(x, norm_ffn[l], w_router[j], moe_w_gate, moe_w_up, moe_w_down, j)
        else:
            return moe_ffn(x, norm_ffn[l], w_router[j], moe_w_gate, moe_w_up, moe_w_down, j,
                           norm_g=norm_final).reshape(bsz, s, d)
    return rmsnorm(x, norm_final, F32).reshape(bsz, s, d)
```

```python
import functools
import math

import jax
import jax.numpy as jnp
from jax import lax
from jax.experimental import pallas as pl
from jax.experimental.pallas import tpu as pltpu

F32 = jnp.float32
BF16 = jnp.bfloat16
U32 = jnp.uint32

NORM_EPS = 1e-6
SUBLN_EPS = 1e-5
CONV_K = 3
POOL_WINDOWS = (2, 4, 8, 16)
DA_HEADS = 8
DA_HEAD_DIM = 128
N_BRANCHES = 3
TOP_K = 2

LANES = 128
BF16_ROWS = 16
MXU_EDGE = 256
VMEM_BYTES = 64 * 1024 * 1024
VMEM_LIMIT = VMEM_BYTES - 8 * 1024 * 1024

HALO = BF16_ROWS
assert HALO >= max(POOL_WINDOWS) and HALO >= CONV_K
LOG2E = math.log2(math.e)
HI16 = 0xFFFF0000


def _tile(dim, pref, unit=LANES):
    if dim <= pref:
        return dim
    t = (pref // unit) * unit
    while t > unit and dim % t:
        t -= unit
    assert dim % t == 0, (dim, pref, unit)
    return t


def _params(sem):
    return pltpu.CompilerParams(dimension_semantics=sem, vmem_limit_bytes=VMEM_LIMIT)


def _resident(block_shape, index_map):
    return pl.BlockSpec(block_shape, index_map, pipeline_mode=pl.Buffered(1))


def _rms(x, g):
    ms = jnp.mean(x * x, axis=-1, keepdims=True)
    return x * lax.rsqrt(ms + NORM_EPS) * g


def _rmsnorm_body(x_ref, g_ref, o_ref):
    o_ref[...] = _rms(x_ref[...], g_ref[...]).astype(o_ref.dtype)


def rmsnorm(x, g, out_dtype):
    n, d = x.shape
    tr = _tile(n, 256, 8)
    return pl.pallas_call(
        _rmsnorm_body,
        grid=(n // tr,),
        in_specs=[pl.BlockSpec((tr, d), lambda i: (i, 0)),
                  pl.BlockSpec((1, d), lambda i: (0, 0))],
        out_specs=pl.BlockSpec((tr, d), lambda i: (i, 0)),
        out_shape=jax.ShapeDtypeStruct((n, d), out_dtype),
        compiler_params=_params(("parallel",)),
        name="rmsnorm",
    )(x, g.reshape(1, d))


def _pack_halves(h):
    half = h.shape[1] // 2
    lo = lax.bitcast_convert_type(h[:, :half].astype(BF16).astype(F32), U32)
    hi = lax.bitcast_convert_type(h[:, half:].astype(BF16).astype(F32), U32)
    return (lo >> 16) | (hi & jnp.uint32(HI16))


def _unpack_halves(p, dtype):
    lo = lax.bitcast_convert_type(p << 16, F32).astype(dtype)
    hi = lax.bitcast_convert_type(p & jnp.uint32(HI16), F32).astype(dtype)
    return lo, hi


def _split_bf16(x):
    hi = x.astype(BF16)
    return hi, (x - hi.astype(F32)).astype(BF16)


def _rmsnorm_router_body(x_ref, g_ref, wr_ref, hp_ref, ei_ref, ew_ref, *, n_experts):
    h = _rms(x_ref[...], g_ref[...])
    hp_ref[...] = _pack_halves(h)
    h_hi, h_lo = _split_bf16(h)
    w_hi, w_lo = _split_bf16(wr_ref[...])
    logits = (jnp.dot(h_hi, w_hi, preferred_element_type=F32) + jnp.dot(h_hi, w_lo, preferred_element_type=F32)
              + jnp.dot(h_lo, w_hi, preferred_element_type=F32))
    lane = lax.broadcasted_iota(jnp.int32, logits.shape, 1)
    neg = jnp.float32(-jnp.inf)
    logits = jnp.where(lane < n_experts, logits, neg)
    m1 = jnp.max(logits, axis=-1, keepdims=True)
    i1 = jnp.min(jnp.where(logits == m1, lane, LANES), axis=-1, keepdims=True)
    rest = jnp.where(lane == i1, neg, logits)
    m2 = jnp.max(rest, axis=-1, keepdims=True)
    i2 = jnp.min(jnp.where(rest == m2, lane, LANES), axis=-1, keepdims=True)
    e2 = jnp.exp(m2 - m1)
    w1 = 1.0 / (1.0 + e2)
    w2 = e2 / (1.0 + e2)
    ei_ref[...] = jnp.where(lane == 0, i1, jnp.where(lane == 1, i2, 0))
    ew_ref[...] = jnp.where(lane == 0, w1, jnp.where(lane == 1, w2, 0.0))


def rmsnorm_router(x, g, w_router):
    n, d = x.shape
    n_experts = w_router.shape[1]
    wr = jnp.pad(w_router, ((0, 0), (0, LANES - n_experts)))
    tr = _tile(n, 256, 8)
    return pl.pallas_call(
        functools.partial(_rmsnorm_router_body, n_experts=n_experts),
        grid=(n // tr,),
        in_specs=[pl.BlockSpec((tr, d), lambda i: (i, 0)),
                  pl.BlockSpec((1, d), lambda i: (0, 0)),
                  pl.BlockSpec((d, LANES), lambda i: (0, 0))],
        out_specs=[pl.BlockSpec((tr, d // 2), lambda i: (i, 0)),
                   pl.BlockSpec((tr, LANES), lambda i: (i, 0)),
                   pl.BlockSpec((tr, LANES), lambda i: (i, 0))],
        out_shape=[jax.ShapeDtypeStruct((n, d // 2), U32),
                   jax.ShapeDtypeStruct((n, LANES), jnp.int32),
                   jax.ShapeDtypeStruct((n, LANES), F32)],
        compiler_params=_params(("parallel",)),
        name="rmsnorm_router",
    )(x, g.reshape(1, d), wr)


def _mm_body(a_ref, w_ref, *rest, has_res):
    r = jnp.dot(a_ref[...], w_ref[...].astype(BF16), preferred_element_type=F32)
    if has_res:
        r = r + rest[0][...]
    rest[-1][...] = r.astype(rest[-1].dtype)


def matmul(a, w, layer, out_dtype, res=None, tm=2048, tn=512):
    m, kd = a.shape
    n = w.shape[2]
    tm, tn = _tile(m, tm), _tile(n, tn)
    in_specs = [_resident((tm, kd), lambda i, j: (i, 0)),
                pl.BlockSpec((None, kd, tn), lambda i, j: (layer, 0, j))]
    args = [a, w]
    if res is not None:
        in_specs.append(pl.BlockSpec((tm, tn), lambda i, j: (i, j)))
        args.append(res)
    return pl.pallas_call(
        functools.partial(_mm_body, has_res=res is not None),
        grid=(m // tm, n // tn),
        in_specs=in_specs,
        out_specs=pl.BlockSpec((tm, tn), lambda i, j: (i, j)),
        out_shape=jax.ShapeDtypeStruct((m, n), out_dtype),
        compiler_params=_params(("parallel", "arbitrary")),
        name="matmul_res" if res is not None else "matmul",
    )(*args)


def _mm_acc_body(a_ref, w_ref, res_ref, o_ref):
    @pl.when(pl.program_id(2) == 0)
    def _():
        o_ref[...] = res_ref[...]

    o_ref[...] += jnp.dot(a_ref[...], w_ref[...].astype(BF16), preferred_element_type=F32)


def matmul_acc(a, w, layer, res, tm=2048, tn=512, tk=1792):
    m, kd = a.shape
    n = w.shape[2]
    tm, tn, tk = _tile(m, tm), _tile(n, tn, MXU_EDGE), _tile(kd, tk, MXU_EDGE)
    return pl.pallas_call(
        _mm_acc_body,
        grid=(m // tm, n // tn, kd // tk),
        in_specs=[pl.BlockSpec((tm, tk), lambda i, j, k: (i, k)),
                  pl.BlockSpec((None, tk, tn), lambda i, j, k: (layer, k, j)),
                  pl.BlockSpec((tm, tn), lambda i, j, k: (i, j))],
        out_specs=pl.BlockSpec((tm, tn), lambda i, j, k: (i, j)),
        out_shape=jax.ShapeDtypeStruct((m, n), F32),
        compiler_params=_params(("parallel", "parallel", "arbitrary")),
        name="matmul_acc",
    )(a, w, res)


def _silu_mul(g, u):
    return g * jax.nn.sigmoid(g) * u


def _glu_body(a_ref, wg_ref, wu_ref, o_ref):
    a = a_ref[...]
    g = jnp.dot(a, wg_ref[...].astype(BF16), preferred_element_type=F32)
    u = jnp.dot(a, wu_ref[...].astype(BF16), preferred_element_type=F32)
    o_ref[...] = _silu_mul(g, u).astype(o_ref.dtype)


def glu_in(a, wg, wu, layer, tm=2048, tn=256):
    m, kd = a.shape
    f = wg.shape[2]
    tm, tn = _tile(m, tm), _tile(f, tn, MXU_EDGE)
    w_spec = pl.BlockSpec((None, kd, tn), lambda i, j: (layer, 0, j))
    return pl.pallas_call(
        _glu_body,
        grid=(m // tm, f // tn),
        in_specs=[pl.BlockSpec((tm, kd), lambda i, j: (i, 0)), w_spec, w_spec],
        out_specs=pl.BlockSpec((tm, tn), lambda i, j: (i, j)),
        out_shape=jax.ShapeDtypeStruct((m, f), BF16),
        compiler_params=_params(("parallel", "arbitrary")),
        name="glu_in",
    )(a, wg, wu)


MERGE_ROWS = 256


def _merge_body(a_ref, p_ref, c_ref, wa_ref, wp_ref, wc_ref, g0_ref, g1_ref, g2_ref, o_ref):
    branches = ((a_ref, wa_ref[...].astype(BF16), g0_ref),
                (p_ref, wp_ref[...].astype(BF16), g1_ref),
                (c_ref, wc_ref[...].astype(BF16), g2_ref))
    rc = min(MERGE_ROWS, o_ref.shape[0])
    for r in range(o_ref.shape[0] // rc):
        rows = slice(r * rc, (r + 1) * rc)
        acc = None
        for x_ref, w, g_ref in branches:
            y = jax.nn.sigmoid(g_ref[rows].astype(F32)) * jnp.dot(x_ref[rows], w, preferred_element_type=F32)
            acc = y if acc is None else acc + y
        o_ref[rows] = acc.astype(o_ref.dtype)


def merge(ya, yp, yc, wa, wp, wc, layer, proj, gate_col, tm=2048, tn=256):
    m = ya.shape[0]
    d = wa.shape[2]
    tm, tn = _tile(m, tm), _tile(d, tn)
    assert gate_col % tn == 0
    gb, nd = gate_col // tn, d // tn

    def x_spec(x):
        return pl.BlockSpec((tm, x.shape[1]), lambda i, j: (i, 0))

    def w_spec(w):
        return pl.BlockSpec((None, w.shape[1], tn), lambda i, j: (layer, 0, j))

    def g_spec(b):
        return pl.BlockSpec((tm, tn), lambda i, j: (i, gb + b * nd + j))

    return pl.pallas_call(
        _merge_body,
        grid=(m // tm, nd),
        in_specs=[x_spec(ya), x_spec(yp), x_spec(yc), w_spec(wa), w_spec(wp), w_spec(wc),
                  g_spec(0), g_spec(1), g_spec(2)],
        out_specs=pl.BlockSpec((tm, tn), lambda i, j: (i, j)),
        out_shape=jax.ShapeDtypeStruct((m, d), BF16),
        compiler_params=_params(("parallel", "parallel")),
        name="merge",
    )(ya, yp, yc, wa, wp, wc, proj, proj, proj)


def _with_halo(prev, cur, first):
    prev = jnp.where(first, 0.0, prev)
    return jnp.concatenate([prev, cur], axis=0)


def _conv_body(u_ref, b_ref, c_ref, up_ref, cp_ref, w_ref, o_ref):
    first = pl.program_id(1) == 0
    z = c_ref[...].astype(F32) * u_ref[...].astype(F32)
    zp = cp_ref[...].astype(F32) * up_ref[...].astype(F32)
    ze = _with_halo(zp, z, first)
    z1 = pltpu.roll(ze, 1, 0)[HALO:]
    z2 = pltpu.roll(ze, 2, 0)[HALO:]
    w = w_ref[...]
    zc = w[0:1] * z2 + w[1:2] * z1 + w[2:3] * z
    o_ref[...] = (b_ref[...].astype(F32) * zc).astype(o_ref.dtype)


def short_conv(proj, conv_w, ts=2048, cw=256):
    bsz, s, _ = proj.shape
    width = conv_w.shape[1]
    ts, cw = _tile(s, ts, HALO), _tile(width, cw)
    nc, hb = width // cw, ts // HALO

    def cur(off):
        return pl.BlockSpec((None, ts, cw), lambda b, i, j: (b, i, off * nc + j))

    def prev(off):
        return pl.BlockSpec((None, HALO, cw), lambda b, i, j: (b, jnp.maximum(i * hb - 1, 0), off * nc + j))

    return pl.pallas_call(
        _conv_body,
        grid=(bsz, s // ts, nc),
        in_specs=[cur(0), cur(1), cur(2), prev(0), prev(2),
                  pl.BlockSpec((CONV_K, cw), lambda b, i, j: (0, j))],
        out_specs=pl.BlockSpec((None, ts, cw), lambda b, i, j: (b, i, j)),
        out_shape=jax.ShapeDtypeStruct((bsz, s, width), BF16),
        compiler_params=_params(("parallel", "parallel", "parallel")),
        name="short_conv",
    )(proj, proj, proj, proj, proj, conv_w)


def _pool_body(u_ref, up_ref, pw_ref, ps_ref, o_ref, *, ts):
    i, g = pl.program_id(1), pl.program_id(2)
    u = u_ref[...].astype(F32)
    ue = _with_halo(up_ref[...].astype(F32), u, i == 0)
    t1 = i * ts + lax.broadcasted_iota(jnp.int32, u.shape, 0) + 1
    for gi, w in enumerate(POOL_WINDOWS):
        assert w & (w - 1) == 0 and w <= HALO

        @pl.when(g == gi)
        def _(w=w):
            s, span = ue, 1
            while span < w:
                s = s + pltpu.roll(s, span, 0)
                span *= 2
            mean = s[HALO:] / jnp.minimum(t1, w).astype(F32)
            mixed = (mean - u).astype(BF16)
            y = jnp.dot(mixed, pw_ref[...].astype(BF16), preferred_element_type=F32)
            o_ref[...] = (y * ps_ref[...]).astype(o_ref.dtype)


def pool_mix(proj, col, pool_w, pool_scale, ts=2048):
    bsz, s, _ = proj.shape
    n_g, gd, _ = pool_w.shape
    assert n_g == len(POOL_WINDOWS) and col % gd == 0
    ts = _tile(s, ts, HALO)
    cb, hb = col // gd, ts // HALO
    return pl.pallas_call(
        functools.partial(_pool_body, ts=ts),
        grid=(bsz, s // ts, n_g),
        in_specs=[pl.BlockSpec((None, ts, gd), lambda b, i, g: (b, i, cb + g)),
                  pl.BlockSpec((None, HALO, gd), lambda b, i, g: (b, jnp.maximum(i * hb - 1, 0), cb + g)),
                  pl.BlockSpec((None, gd, gd), lambda b, i, g: (g, 0, 0)),
                  pl.BlockSpec((1, gd), lambda b, i, g: (0, g))],
        out_specs=pl.BlockSpec((None, ts, gd), lambda b, i, g: (b, i, g)),
        out_shape=jax.ShapeDtypeStruct((bsz, s, n_g * gd), BF16),
        compiler_params=_params(("parallel", "parallel", "parallel")),
        name="pool_mix",
    )(proj, proj, pool_w, pool_scale.reshape(1, n_g * gd))


CHUNK_GROUP = 2


def _for_chunks(n, fn):
    def group(jj, carry):
        for u in range(CHUNK_GROUP):
            fn(CHUNK_GROUP * jj + u)
        return carry

    def single(j, carry):
        fn(j)
        return carry

    lax.fori_loop(0, n // CHUNK_GROUP, group, 0)
    lax.fori_loop(n - n % CHUNK_GROUP, n, single, 0)


def _attn_body(lam_ref, g_ref, q_ref, k_ref, v_ref, o_ref, s_ref, m_ref, l_ref, acc_ref, *, tq, lambda_init):
    dh = DA_HEAD_DIM
    i = pl.program_id(2)
    maps = range(2)
    slabs = [slice(b * LANES, (b + 1) * LANES) for b in range(tq // LANES)]
    q = (q_ref[...].astype(F32) * (dh ** -0.5 * LOG2E)).astype(BF16)

    def rows(j):
        return pl.ds(pl.multiple_of(j * tq, tq), tq)

    def scores(j, c):
        cols = slice(c * dh, (c + 1) * dh)
        return lax.dot_general(q[:, cols], k_ref[rows(j), cols], (((1,), (1,)), ((), ())),
                               preferred_element_type=F32)

    def keep(j, c, s):
        s_ref[c, j] = s
        m = m_ref[c]
        for sl in slabs:
            m = jnp.maximum(m, s[:, sl])
        m_ref[c] = m

    def pass1(j):
        for c in maps:
            keep(j, c, scores(j, c))

    m_ref[...] = jnp.full_like(m_ref, -jnp.inf)
    _for_chunks(i, pass1)
    row = lax.broadcasted_iota(jnp.int32, (tq, tq), 0)
    col = lax.broadcasted_iota(jnp.int32, (tq, tq), 1)
    for c in maps:
        keep(i, c, jnp.where(col <= row, scores(i, c), -jnp.inf))
    mb = [jnp.broadcast_to(jnp.max(m_ref[c], axis=-1, keepdims=True), (tq, LANES)) for c in maps]

    def pass2(j):
        p_maps = []
        for c in maps:
            s = s_ref[c, j]
            l = l_ref[c]
            ps = []
            for sl in slabs:
                p = jnp.exp2(s[:, sl] - mb[c])
                l = l + p
                ps.append(p.astype(BF16))
            l_ref[c] = l
            p_maps.append(jnp.concatenate(ps, axis=1))
        acc_ref[...] += jnp.dot(jnp.concatenate(p_maps, axis=0), v_ref[rows(j), :], preferred_element_type=F32)

    l_ref[...] = jnp.zeros_like(l_ref)
    acc_ref[...] = jnp.zeros_like(acc_ref)
    _for_chunks(i + 1, pass2)
    outs = [acc_ref[c * tq:(c + 1) * tq] / jnp.sum(l_ref[c], axis=-1, keepdims=True) for c in maps]

    lp = lam_ref[...]
    lam = (jnp.exp(jnp.sum(lp[0:1] * lp[1:2], axis=-1, keepdims=True))
           - jnp.exp(jnp.sum(lp[2:3] * lp[3:4], axis=-1, keepdims=True)) + lambda_init)
    o = outs[0] - lam * outs[1]
    o = o * lax.rsqrt(jnp.mean(o * o, axis=-1, keepdims=True) + SUBLN_EPS)
    o_ref[...] = (o * g_ref[...] * (1.0 - lambda_init)).astype(o_ref.dtype)


def diff_attention(proj, col, lam, subln_g, lambda_init, tq=512):
    bsz, s, _ = proj.shape
    hw = 2 * DA_HEAD_DIM
    assert col % hw == 0
    tq = _tile(s, tq)
    qb = col // hw
    kb, vb = qb + DA_HEADS, qb + 2 * DA_HEADS
    return pl.pallas_call(
        functools.partial(_attn_body, tq=tq, lambda_init=lambda_init),
        grid=(bsz, DA_HEADS, s // tq),
        in_specs=[pl.BlockSpec((4, DA_HEAD_DIM), lambda b, h, i: (0, 0)),
                  pl.BlockSpec((1, hw), lambda b, h, i: (0, 0)),
                  pl.BlockSpec((None, tq, hw), lambda b, h, i: (b, i, qb + h)),
                  pl.BlockSpec((None, s, hw), lambda b, h, i: (b, 0, kb + h)),
                  pl.BlockSpec((None, s, hw), lambda b, h, i: (b, 0, vb + h))],
        out_specs=pl.BlockSpec((None, tq, hw), lambda b, h, i: (b, i, h)),
        out_shape=jax.ShapeDtypeStruct((bsz, s, DA_HEADS * hw), BF16),
        scratch_shapes=[pltpu.VMEM((2, s // tq, tq, tq), F32), pltpu.VMEM((2, tq, LANES), F32),
                        pltpu.VMEM((2, tq, LANES), F32), pltpu.VMEM((2 * tq, hw), F32)],
        compiler_params=_params(("parallel", "parallel", "arbitrary")),
        name="diff_attention",
    )(lam, subln_g.reshape(1, hw), proj, proj, proj)


MOE_ROWS = 512
MOE_TOKENS = 256


def _route(ei, n_e):
    n = ei.shape[0]
    e = ei[:, :TOP_K]
    onehot = (e[:, :, None] == jnp.arange(n_e, dtype=jnp.int32)).astype(jnp.int32).reshape(n * TOP_K, n_e)
    incl = jnp.cumsum(onehot, axis=0)
    rank = jnp.sum((incl - onehot) * onehot, axis=1)
    tiles = (incl[-1] + MOE_ROWS - 1) // MOE_ROWS
    tile_end = jnp.cumsum(tiles)
    start = (tile_end - tiles) * MOE_ROWS
    dest = (jnp.sum(onehot * start[None, :], axis=1) + rank).reshape(n, TOP_K)
    max_tiles = (n * TOP_K) // MOE_ROWS + n_e
    n_used = tile_end[-1:]
    t = jnp.minimum(jnp.arange(max_tiles, dtype=jnp.int32), n_used[0] - 1)
    tile_expert = jnp.sum((t[:, None] >= tile_end[None, :]).astype(jnp.int32), axis=1)
    return dest.astype(jnp.int32), tile_expert.astype(jnp.int32), n_used.astype(jnp.int32), max_tiles


def _dispatch_body(dest_ref, hp_ref, init_ref, hs_ref, sem, *, tt):
    del init_ref

    def row_copy(t, k):
        return pltpu.make_async_copy(hp_ref.at[pl.ds(t, 1)], hs_ref.at[pl.ds(dest_ref[0, TOP_K * t + k], 1)], sem)

    def start(t, carry):
        for k in range(TOP_K):
            row_copy(t, k).start()
        return carry

    def wait(t, carry):
        for k in range(TOP_K):
            row_copy(t, k).wait()
        return carry

    lax.fori_loop(0, tt, start, 0)
    lax.fori_loop(0, tt, wait, 0)


def moe_dispatch(hp, dest, n_rows):
    n, w = hp.shape
    tt = _tile(n, MOE_TOKENS, 8)
    return pl.pallas_call(
        functools.partial(_dispatch_body, tt=tt),
        grid=(n // tt,),
        in_specs=[pl.BlockSpec((None, 1, TOP_K * tt), lambda i: (i, 0, 0), memory_space=pltpu.SMEM),
                  pl.BlockSpec((tt, w), lambda i: (i, 0)),
                  pl.BlockSpec(memory_space=pl.ANY)],
        out_specs=pl.BlockSpec(memory_space=pl.ANY),
        out_shape=jax.ShapeDtypeStruct((n_rows, w), U32),
        scratch_shapes=[pltpu.SemaphoreType.DMA(())],
        input_output_aliases={2: 0},
        compiler_params=_params(("arbitrary",)),
        name="moe_dispatch",
    )(dest.reshape(n // tt, 1, TOP_K * tt), hp, jnp.zeros((n_rows, w), U32))


def _zero_unused_tile(nu_ref, o_ref):
    @pl.when(pl.program_id(1) >= nu_ref[0])
    def _():
        o_ref[...] = jnp.zeros_like(o_ref)


def _expert_changed(te_ref, r):
    return (r == 0) | (te_ref[r] != te_ref[jnp.maximum(r - 1, 0)])


def _moe_glu_body(te_ref, nu_ref, a_ref, wg_ref, wu_ref, o_ref, wgb_ref, wub_ref):
    r = pl.program_id(1)
    _zero_unused_tile(nu_ref, o_ref)

    @pl.when(_expert_changed(te_ref, r))
    def _():
        wgb_ref[...] = wg_ref[...].astype(BF16)
        wub_ref[...] = wu_ref[...].astype(BF16)

    @pl.when(r < nu_ref[0])
    def _():
        lo, hi = _unpack_halves(a_ref[...], BF16)
        half = lo.shape[1]

        def proj(w_ref):
            return (jnp.dot(lo, w_ref[:half], preferred_element_type=F32)
                    + jnp.dot(hi, w_ref[half:], preferred_element_type=F32))

        o_ref[...] = _silu_mul(proj(wgb_ref), proj(wub_ref)).astype(o_ref.dtype)


def moe_glu(hs, wg, wu, layer, tile_expert, n_used, tn=512):
    rows, half = hs.shape
    f = wg.shape[3]
    tn = _tile(f, tn, MXU_EDGE)
    n_tiles = rows // MOE_ROWS

    def row_tile(r, nu):
        return jnp.minimum(r, nu[0] - 1)

    w_spec = pl.BlockSpec((None, None, 2 * half, tn), lambda j, r, te, nu: (layer, te[r], 0, j))
    return pl.pallas_call(
        _moe_glu_body,
        grid_spec=pltpu.PrefetchScalarGridSpec(
            num_scalar_prefetch=2,
            grid=(f // tn, n_tiles),
            in_specs=[pl.BlockSpec((MOE_ROWS, half), lambda j, r, te, nu: (row_tile(r, nu), 0)), w_spec, w_spec],
            out_specs=pl.BlockSpec((MOE_ROWS, tn), lambda j, r, te, nu: (r, j)),
            scratch_shapes=[pltpu.VMEM((2 * half, tn), BF16), pltpu.VMEM((2 * half, tn), BF16)]),
        out_shape=jax.ShapeDtypeStruct((rows, f), BF16),
        compiler_params=_params(("arbitrary", "arbitrary")),
        name="moe_glu",
    )(tile_expert, n_used, hs, wg, wu)


def _moe_down_body(te_ref, nu_ref, a_ref, wlo_ref, whi_ref, o_ref):
    del te_ref
    _zero_unused_tile(nu_ref, o_ref)

    @pl.when(pl.program_id(1) < nu_ref[0])
    def _():
        a = a_ref[...]
        ys = [jnp.dot(a, w_ref[...].astype(BF16), preferred_element_type=F32) for w_ref in (wlo_ref, whi_ref)]
        o_ref[...] = _pack_halves(jnp.concatenate(ys, axis=1))


def moe_down(act, wd, layer, tile_expert, n_used, tn=512):
    rows, f = act.shape
    half = wd.shape[3] // 2
    tn = _tile(half, tn)
    n_tiles, nh = rows // MOE_ROWS, half // tn

    def row_tile(r, nu):
        return jnp.minimum(r, nu[0] - 1)

    def w_spec(hi):
        return pl.BlockSpec((None, None, f, tn), lambda j, r, te, nu: (layer, te[r], 0, j + hi * nh))

    return pl.pallas_call(
        _moe_down_body,
        grid_spec=pltpu.PrefetchScalarGridSpec(
            num_scalar_prefetch=2,
            grid=(nh, n_tiles),
            in_specs=[pl.BlockSpec((MOE_ROWS, f), lambda j, r, te, nu: (row_tile(r, nu), 0)), w_spec(0), w_spec(1)],
            out_specs=pl.BlockSpec((MOE_ROWS, tn), lambda j, r, te, nu: (r, j))),
        out_shape=jax.ShapeDtypeStruct((rows, half), U32),
        compiler_params=_params(("arbitrary", "arbitrary")),
        name="moe_down",
    )(tile_expert, n_used, act, wd, wd)


def _combine_body(dest_ref, next_ref, ew_ref, x_ref, *rest, tt, has_norm):
    if has_norm:
        g_ref, ys_ref, o_ref, ybuf, sem = rest
    else:
        ys_ref, o_ref, ybuf, sem = rest
    i, n_steps = pl.program_id(0), pl.num_programs(0)
    slot = i % 2

    def row_copy(d_ref, sl, t, k):
        return pltpu.make_async_copy(ys_ref.at[pl.ds(d_ref[0, TOP_K * t + k], 1)],
                                     ybuf.at[sl, k, pl.ds(t, 1)], sem.at[sl])

    def start_all(d_ref, sl):
        def start(t, carry):
            for k in range(TOP_K):
                row_copy(d_ref, sl, t, k).start()
            return carry

        lax.fori_loop(0, tt, start, 0)

    @pl.when(i == 0)
    def _():
        start_all(dest_ref, slot)

    @pl.when(i + 1 < n_steps)
    def _():
        start_all(next_ref, 1 - slot)

    def wait(t, carry):
        for k in range(TOP_K):
            row_copy(dest_ref, slot, t, k).wait()
        return carry

    lax.fori_loop(0, tt, wait, 0)
    ew = ew_ref[...]
    y = x_ref[...]
    for k in range(TOP_K):
        y = y + ew[:, k:k + 1] * jnp.concatenate(_unpack_halves(ybuf[slot, k], F32), axis=1)
    o_ref[...] = _rms(y, g_ref[...]) if has_norm else y


def moe_combine(x, ys, dest, ew, norm_g=None):
    n, d = x.shape
    tt = _tile(n, MOE_TOKENS, 8)
    n_steps = n // tt
    dest = dest.reshape(n_steps, 1, TOP_K * tt)

    def d_spec(ahead):
        return pl.BlockSpec((None, 1, TOP_K * tt), lambda i: (jnp.minimum(i + ahead, n_steps - 1), 0, 0),
                            memory_space=pltpu.SMEM)

    in_specs = [d_spec(0), d_spec(1),
                pl.BlockSpec((tt, LANES), lambda i: (i, 0)),
                pl.BlockSpec((tt, d), lambda i: (i, 0))]
    args = [dest, dest, ew, x]
    if norm_g is not None:
        in_specs.append(pl.BlockSpec((1, d), lambda i: (0, 0)))
        args.append(norm_g.reshape(1, d))
    return pl.pallas_call(
        functools.partial(_combine_body, tt=tt, has_norm=norm_g is not None),
        grid=(n_steps,),
        in_specs=in_specs + [pl.BlockSpec(memory_space=pl.ANY)],
        out_specs=pl.BlockSpec((tt, d), lambda i: (i, 0)),
        out_shape=jax.ShapeDtypeStruct((n, d), F32),
        scratch_shapes=[pltpu.VMEM((2, TOP_K, tt, d // 2), U32), pltpu.SemaphoreType.DMA((2,))],
        compiler_params=_params(("arbitrary",)),
        name="moe_combine",
    )(*args, ys)


def moe_ffn(x, g, w_router, wg, wu, wd, layer, norm_g=None):
    n_e = wg.shape[1]
    hp, ei, ew = rmsnorm_router(x, g, w_router)
    dest, tile_expert, n_used, max_tiles = _route(ei, n_e)
    hs = moe_dispatch(hp, dest, max_tiles * MOE_ROWS)
    act = moe_glu(hs, wg, wu, layer, tile_expert, n_used)
    ys = moe_down(act, wd, layer, tile_expert, n_used)
    return moe_combine(x, ys, dest, ew, norm_g)


def _mixer(x, h, l, w_in, conv_w, w_a_out, pool_w, pool_scale, w_p_out, lam, subln, w_c_out, w_o,
           lambda_init, bsz, s):
    n, d = x.shape
    cw, pw, aw = conv_w.shape[2], pool_scale.shape[1], w_c_out.shape[1]
    proj = matmul(h, w_in, l, BF16)
    proj3 = proj.reshape(bsz, s, proj.shape[1])
    ya = short_conv(proj3, conv_w[l]).reshape(n, cw)
    yp = pool_mix(proj3, 3 * cw, pool_w[l], pool_scale[l]).reshape(n, pw)
    yc = diff_attention(proj3, 3 * cw + pw, lam[l], subln[l], lambda_init).reshape(n, aw)
    merged = merge(ya, yp, yc, w_a_out, w_p_out, w_c_out, l, proj, 3 * cw + pw + 3 * aw)
    return matmul(merged, w_o, l, F32, res=x, tn=256)


def kernel(x, norm_mix, w_in, conv_w, w_a_out, pool_w, pool_scale, w_p_out, lam, subln, w_c_out, w_o,
           norm_ffn, ffn_w_gate, ffn_w_up, ffn_w_down, w_router, moe_w_gate, moe_w_up, moe_w_down,
           norm_final):
    bsz, s, d = x.shape
    depth = norm_mix.shape[0]
    n = bsz * s
    x = x.reshape(n, d)
    for l in range(depth):
        lambda_init = 0.8 - 0.6 * math.exp(-0.3 * l)
        h = rmsnorm(x, norm_mix[l], BF16)
        x = _mixer(x, h, l, w_in, conv_w, w_a_out, pool_w, pool_scale, w_p_out, lam, subln, w_c_out, w_o,
                   lambda_init, bsz, s)
        j = l // 2
        if l % 2 == 0:
            h = rmsnorm(x, norm_ffn[l], BF16)
            act = glu_in(h, ffn_w_gate, ffn_w_up, j)
            x = matmul_acc(act, ffn_w_down, j, x)
        elif l < depth - 1:
            x = moe_ffn(x, norm_ffn[l], w_router[j], moe_w_gate, moe_w_up, moe_w_down, j)
        else:
            return moe_ffn(x, norm_ffn[l], w_router[j], moe_w_gate, moe_w_up, moe_w_down, j,
                           norm_g=norm_final).reshape(bsz, s, d)
    return rmsnorm(x, norm_final, F32).reshape(bsz, s, d)
```

```python
import functools
import math

import jax
import jax.numpy as jnp
from jax import lax
from jax.experimental import pallas as pl
from jax.experimental.pallas import tpu as pltpu

F32 = jnp.float32
BF16 = jnp.bfloat16
U32 = jnp.uint32

NORM_EPS = 1e-6
SUBLN_EPS = 1e-5
CONV_K = 3
POOL_WINDOWS = (2, 4, 8, 16)
DA_HEADS = 8
DA_HEAD_DIM = 128
N_BRANCHES = 3
TOP_K = 2

LANES = 128
BF16_ROWS = 16
MXU_EDGE = 256
VMEM_BYTES = 64 * 1024 * 1024
VMEM_LIMIT = VMEM_BYTES - 8 * 1024 * 1024

HALO = BF16_ROWS
assert HALO >= max(POOL_WINDOWS) and HALO >= CONV_K
LOG2E = math.log2(math.e)
HI16 = 0xFFFF0000


def _tile(dim, pref, unit=LANES):
    if dim <= pref:
        return dim
    t = (pref // unit) * unit
    while t > unit and dim % t:
        t -= unit
    assert dim % t == 0, (dim, pref, unit)
    return t


def _params(sem):
    return pltpu.CompilerParams(dimension_semantics=sem, vmem_limit_bytes=VMEM_LIMIT)


def _resident(block_shape, index_map):
    return pl.BlockSpec(block_shape, index_map, pipeline_mode=pl.Buffered(1))


def _rms(x, g):
    ms = jnp.mean(x * x, axis=-1, keepdims=True)
    return x * lax.rsqrt(ms + NORM_EPS) * g


def _rmsnorm_body(x_ref, g_ref, o_ref):
    o_ref[...] = _rms(x_ref[...], g_ref[...]).astype(o_ref.dtype)


def rmsnorm(x, g, out_dtype):
    n, d = x.shape
    tr = _tile(n, 256, 8)
    return pl.pallas_call(
        _rmsnorm_body,
        grid=(n // tr,),
        in_specs=[pl.BlockSpec((tr, d), lambda i: (i, 0)),
                  pl.BlockSpec((1, d), lambda i: (0, 0))],
        out_specs=pl.BlockSpec((tr, d), lambda i: (i, 0)),
        out_shape=jax.ShapeDtypeStruct((n, d), out_dtype),
        compiler_params=_params(("parallel",)),
        name="rmsnorm",
    )(x, g.reshape(1, d))


def _pack_halves(h):
    half = h.shape[1] // 2
    lo = lax.bitcast_convert_type(h[:, :half].astype(BF16).astype(F32), U32)
    hi = lax.bitcast_convert_type(h[:, half:].astype(BF16).astype(F32), U32)
    return (lo >> 16) | (hi & jnp.uint32(HI16))


def _unpack_halves(p):
    lo = lax.bitcast_convert_type(p << 16, F32).astype(BF16)
    hi = lax.bitcast_convert_type(p & jnp.uint32(HI16), F32).astype(BF16)
    return lo, hi


def _split_bf16(x):
    hi = x.astype(BF16)
    return hi, (x - hi.astype(F32)).astype(BF16)


def _rmsnorm_router_body(x_ref, g_ref, wr_ref, hp_ref, ei_ref, ew_ref, *, n_experts):
    h = _rms(x_ref[...], g_ref[...])
    hp_ref[...] = _pack_halves(h)
    h_hi, h_lo = _split_bf16(h)
    w_hi, w_lo = _split_bf16(wr_ref[...])
    logits = (jnp.dot(h_hi, w_hi, preferred_element_type=F32) + jnp.dot(h_hi, w_lo, preferred_element_type=F32)
              + jnp.dot(h_lo, w_hi, preferred_element_type=F32))
    lane = lax.broadcasted_iota(jnp.int32, logits.shape, 1)
    neg = jnp.float32(-jnp.inf)
    logits = jnp.where(lane < n_experts, logits, neg)
    m1 = jnp.max(logits, axis=-1, keepdims=True)
    i1 = jnp.min(jnp.where(logits == m1, lane, LANES), axis=-1, keepdims=True)
    rest = jnp.where(lane == i1, neg, logits)
    m2 = jnp.max(rest, axis=-1, keepdims=True)
    i2 = jnp.min(jnp.where(rest == m2, lane, LANES), axis=-1, keepdims=True)
    e2 = jnp.exp(m2 - m1)
    w1 = 1.0 / (1.0 + e2)
    w2 = e2 / (1.0 + e2)
    ei_ref[...] = jnp.where(lane == 0, i1, jnp.where(lane == 1, i2, 0))
    ew_ref[...] = jnp.where(lane == 0, w1, jnp.where(lane == 1, w2, 0.0))


def rmsnorm_router(x, g, w_router):
    n, d = x.shape
    n_experts = w_router.shape[1]
    wr = jnp.pad(w_router, ((0, 0), (0, LANES - n_experts)))
    tr = _tile(n, 256, 8)
    return pl.pallas_call(
        functools.partial(_rmsnorm_router_body, n_experts=n_experts),
        grid=(n // tr,),
        in_specs=[pl.BlockSpec((tr, d), lambda i: (i, 0)),
                  pl.BlockSpec((1, d), lambda i: (0, 0)),
                  pl.BlockSpec((d, LANES), lambda i: (0, 0))],
        out_specs=[pl.BlockSpec((tr, d // 2), lambda i: (i, 0)),
                   pl.BlockSpec((tr, LANES), lambda i: (i, 0)),
                   pl.BlockSpec((tr, LANES), lambda i: (i, 0))],
        out_shape=[jax.ShapeDtypeStruct((n, d // 2), U32),
                   jax.ShapeDtypeStruct((n, LANES), jnp.int32),
                   jax.ShapeDtypeStruct((n, LANES), F32)],
        compiler_params=_params(("parallel",)),
        name="rmsnorm_router",
    )(x, g.reshape(1, d), wr)


ACC_ROWS = 512


def _mm_body(a_ref, w_ref, *rest, has_res):
    o_ref = rest[-1]
    w = w_ref[...].astype(BF16)
    rc = min(ACC_ROWS, o_ref.shape[0])
    for c in range(o_ref.shape[0] // rc):
        rows = slice(c * rc, (c + 1) * rc)
        r = jnp.dot(a_ref[rows], w, preferred_element_type=F32)
        if has_res:
            r = r + rest[0][rows]
        o_ref[rows] = r.astype(o_ref.dtype)


def matmul(a, w, layer, out_dtype, res=None, tm=2048, tn=512, resident_a=False):
    m, kd = a.shape
    n = w.shape[2]
    tm, tn = _tile(m, tm), _tile(n, tn)
    a_spec = _resident if resident_a else pl.BlockSpec
    in_specs = [a_spec((tm, kd), lambda i, j: (i, 0)),
                pl.BlockSpec((None, kd, tn), lambda i, j: (layer, 0, j))]
    args = [a, w]
    if res is not None:
        in_specs.append(pl.BlockSpec((tm, tn), lambda i, j: (i, j)))
        args.append(res)
    return pl.pallas_call(
        functools.partial(_mm_body, has_res=res is not None),
        grid=(m // tm, n // tn),
        in_specs=in_specs,
        out_specs=pl.BlockSpec((tm, tn), lambda i, j: (i, j)),
        out_shape=jax.ShapeDtypeStruct((m, n), out_dtype),
        compiler_params=_params(("parallel", "arbitrary")),
        name="matmul_res" if res is not None else "matmul",
    )(*args)


def _mm_acc_body(a_ref, w_ref, res_ref, o_ref):
    @pl.when(pl.program_id(2) == 0)
    def _():
        o_ref[...] = res_ref[...]

    w = w_ref[...].astype(BF16)
    rc = min(ACC_ROWS, o_ref.shape[0])
    for r in range(o_ref.shape[0] // rc):
        rows = slice(r * rc, (r + 1) * rc)
        o_ref[rows] += jnp.dot(a_ref[rows], w, preferred_element_type=F32)


def matmul_acc(a, w, layer, res, tm=2048, tn=1024, tk=1024):
    m, kd = a.shape
    n = w.shape[2]
    tm, tn, tk = _tile(m, tm), _tile(n, tn, MXU_EDGE), _tile(kd, tk, MXU_EDGE)
    return pl.pallas_call(
        _mm_acc_body,
        grid=(m // tm, n // tn, kd // tk),
        in_specs=[pl.BlockSpec((tm, tk), lambda i, j, k: (i, k)),
                  pl.BlockSpec((None, tk, tn), lambda i, j, k: (layer, k, j)),
                  pl.BlockSpec((tm, tn), lambda i, j, k: (i, j))],
        out_specs=pl.BlockSpec((tm, tn), lambda i, j, k: (i, j)),
        out_shape=jax.ShapeDtypeStruct((m, n), F32),
        compiler_params=_params(("parallel", "parallel", "arbitrary")),
        name="matmul_acc",
    )(a, w, res)


def _silu_mul(g, u):
    return g * jax.nn.sigmoid(g) * u


def _glu_body(a_ref, wg_ref, wu_ref, o_ref):
    wg, wu = wg_ref[...].astype(BF16), wu_ref[...].astype(BF16)
    rc = min(ACC_ROWS, o_ref.shape[0])
    for c in range(o_ref.shape[0] // rc):
        rows = slice(c * rc, (c + 1) * rc)
        a = a_ref[rows]
        g = jnp.dot(a, wg, preferred_element_type=F32)
        u = jnp.dot(a, wu, preferred_element_type=F32)
        o_ref[rows] = _silu_mul(g, u).astype(o_ref.dtype)


def glu_in(a, wg, wu, layer, tm=2048, tn=256):
    m, kd = a.shape
    f = wg.shape[2]
    tm, tn = _tile(m, tm), _tile(f, tn, MXU_EDGE)
    w_spec = pl.BlockSpec((None, kd, tn), lambda i, j: (layer, 0, j))
    return pl.pallas_call(
        _glu_body,
        grid=(m // tm, f // tn),
        in_specs=[pl.BlockSpec((tm, kd), lambda i, j: (i, 0)), w_spec, w_spec],
        out_specs=pl.BlockSpec((tm, tn), lambda i, j: (i, j)),
        out_shape=jax.ShapeDtypeStruct((m, f), BF16),
        compiler_params=_params(("parallel", "arbitrary")),
        name="glu_in",
    )(a, wg, wu)


MERGE_ROWS = 256


def _merge_body(a_ref, p_ref, c_ref, wa_ref, wp_ref, wc_ref, g0_ref, g1_ref, g2_ref, o_ref):
    branches = ((a_ref, wa_ref[...].astype(BF16), g0_ref),
                (p_ref, wp_ref[...].astype(BF16), g1_ref),
                (c_ref, wc_ref[...].astype(BF16), g2_ref))
    rc = min(MERGE_ROWS, o_ref.shape[0])
    for r in range(o_ref.shape[0] // rc):
        rows = slice(r * rc, (r + 1) * rc)
        acc = None
        for x_ref, w, g_ref in branches:
            y = jax.nn.sigmoid(g_ref[rows].astype(F32)) * jnp.dot(x_ref[rows], w, preferred_element_type=F32)
            acc = y if acc is None else acc + y
        o_ref[rows] = acc.astype(o_ref.dtype)


def merge(ya, yp, yc, wa, wp, wc, layer, proj, gate_col, tm=2048, tn=256):
    m = ya.shape[0]
    d = wa.shape[2]
    tm, tn = _tile(m, tm), _tile(d, tn)
    assert gate_col % tn == 0
    gb, nd = gate_col // tn, d // tn

    def x_spec(x):
        return pl.BlockSpec((tm, x.shape[1]), lambda i, j: (i, 0))

    def w_spec(w):
        return pl.BlockSpec((None, w.shape[1], tn), lambda i, j: (layer, 0, j))

    def g_spec(b):
        return pl.BlockSpec((tm, tn), lambda i, j: (i, gb + b * nd + j))

    return pl.pallas_call(
        _merge_body,
        grid=(m // tm, nd),
        in_specs=[x_spec(ya), x_spec(yp), x_spec(yc), w_spec(wa), w_spec(wp), w_spec(wc),
                  g_spec(0), g_spec(1), g_spec(2)],
        out_specs=pl.BlockSpec((tm, tn), lambda i, j: (i, j)),
        out_shape=jax.ShapeDtypeStruct((m, d), BF16),
        compiler_params=_params(("parallel", "parallel")),
        name="merge",
    )(ya, yp, yc, wa, wp, wc, proj, proj, proj)


def _with_halo(prev, cur, first):
    prev = jnp.where(first, 0.0, prev)
    return jnp.concatenate([prev, cur], axis=0)


def _conv_body(u_ref, b_ref, c_ref, up_ref, cp_ref, w_ref, o_ref):
    first = pl.program_id(1) == 0
    z = c_ref[...].astype(F32) * u_ref[...].astype(F32)
    zp = cp_ref[...].astype(F32) * up_ref[...].astype(F32)
    ze = _with_halo(zp, z, first)
    z1 = pltpu.roll(ze, 1, 0)[HALO:]
    z2 = pltpu.roll(ze, 2, 0)[HALO:]
    w = w_ref[...]
    zc = w[0:1] * z2 + w[1:2] * z1 + w[2:3] * z
    o_ref[...] = (b_ref[...].astype(F32) * zc).astype(o_ref.dtype)


def short_conv(proj, conv_w, ts=2048, cw=256):
    bsz, s, _ = proj.shape
    width = conv_w.shape[1]
    ts, cw = _tile(s, ts, HALO), _tile(width, cw)
    nc, hb = width // cw, ts // HALO

    def cur(off):
        return pl.BlockSpec((None, ts, cw), lambda b, i, j: (b, i, off * nc + j))

    def prev(off):
        return pl.BlockSpec((None, HALO, cw), lambda b, i, j: (b, jnp.maximum(i * hb - 1, 0), off * nc + j))

    return pl.pallas_call(
        _conv_body,
        grid=(bsz, s // ts, nc),
        in_specs=[cur(0), cur(1), cur(2), prev(0), prev(2),
                  pl.BlockSpec((CONV_K, cw), lambda b, i, j: (0, j))],
        out_specs=pl.BlockSpec((None, ts, cw), lambda b, i, j: (b, i, j)),
        out_shape=jax.ShapeDtypeStruct((bsz, s, width), BF16),
        compiler_params=_params(("parallel", "parallel", "parallel")),
        name="short_conv",
    )(proj, proj, proj, proj, proj, conv_w)


def _pool_body(u_ref, up_ref, pw_ref, ps_ref, o_ref, *, ts):
    i, g = pl.program_id(1), pl.program_id(2)
    u = u_ref[...].astype(F32)
    ue = _with_halo(up_ref[...].astype(F32), u, i == 0)
    t1 = i * ts + lax.broadcasted_iota(jnp.int32, u.shape, 0) + 1
    for gi, w in enumerate(POOL_WINDOWS):
        assert w & (w - 1) == 0 and w <= HALO

        @pl.when(g == gi)
        def _(w=w):
            s, span = ue, 1
            while span < w:
                s = s + pltpu.roll(s, span, 0)
                span *= 2
            mean = s[HALO:] / jnp.minimum(t1, w).astype(F32)
            mixed = (mean - u).astype(BF16)
            y = jnp.dot(mixed, pw_ref[...].astype(BF16), preferred_element_type=F32)
            o_ref[...] = (y * ps_ref[...]).astype(o_ref.dtype)


def pool_mix(proj, col, pool_w, pool_scale, ts=2048):
    bsz, s, _ = proj.shape
    n_g, gd, _ = pool_w.shape
    assert n_g == len(POOL_WINDOWS) and col % gd == 0
    ts = _tile(s, ts, HALO)
    cb, hb = col // gd, ts // HALO
    return pl.pallas_call(
        functools.partial(_pool_body, ts=ts),
        grid=(bsz, s // ts, n_g),
        in_specs=[pl.BlockSpec((None, ts, gd), lambda b, i, g: (b, i, cb + g)),
                  pl.BlockSpec((None, HALO, gd), lambda b, i, g: (b, jnp.maximum(i * hb - 1, 0), cb + g)),
                  pl.BlockSpec((None, gd, gd), lambda b, i, g: (g, 0, 0)),
                  pl.BlockSpec((1, gd), lambda b, i, g: (0, g))],
        out_specs=pl.BlockSpec((None, ts, gd), lambda b, i, g: (b, i, g)),
        out_shape=jax.ShapeDtypeStruct((bsz, s, n_g * gd), BF16),
        compiler_params=_params(("parallel", "parallel", "parallel")),
        name="pool_mix",
    )(proj, proj, pool_w, pool_scale.reshape(1, n_g * gd))


CHUNK_GROUP = 2


def _for_chunks(n, fn):
    def group(jj, carry):
        for u in range(CHUNK_GROUP):
            fn(CHUNK_GROUP * jj + u)
        return carry

    def single(j, carry):
        fn(j)
        return carry

    lax.fori_loop(0, n // CHUNK_GROUP, group, 0)
    lax.fori_loop(n - n % CHUNK_GROUP, n, single, 0)


def _attn_body(lam_ref, g_ref, q_ref, k_ref, v_ref, o_ref, s_ref, m_ref, l_ref, acc_ref, *, tq, lambda_init):
    dh = DA_HEAD_DIM
    i = pl.program_id(2)
    maps = range(2)
    slabs = [slice(b * LANES, (b + 1) * LANES) for b in range(tq // LANES)]
    q = (q_ref[...].astype(F32) * (dh ** -0.5 * LOG2E)).astype(BF16)

    def rows(j):
        return pl.ds(pl.multiple_of(j * tq, tq), tq)

    def scores(j, c):
        cols = slice(c * dh, (c + 1) * dh)
        return lax.dot_general(q[:, cols], k_ref[rows(j), cols], (((1,), (1,)), ((), ())),
                               preferred_element_type=F32)

    def keep(j, c, s):
        s_ref[c, j] = s
        m = m_ref[c]
        for sl in slabs:
            m = jnp.maximum(m, s[:, sl])
        m_ref[c] = m

    def pass1(j):
        for c in maps:
            keep(j, c, scores(j, c))

    m_ref[...] = jnp.full_like(m_ref, -jnp.inf)
    _for_chunks(i, pass1)
    row = lax.broadcasted_iota(jnp.int32, (tq, tq), 0)
    col = lax.broadcasted_iota(jnp.int32, (tq, tq), 1)
    for c in maps:
        keep(i, c, jnp.where(col <= row, scores(i, c), -jnp.inf))
    mb = [jnp.broadcast_to(jnp.max(m_ref[c], axis=-1, keepdims=True), (tq, LANES)) for c in maps]

    def pass2(j):
        p_maps = []
        for c in maps:
            s = s_ref[c, j]
            l = l_ref[c]
            ps = []
            for sl in slabs:
                p = jnp.exp2(s[:, sl] - mb[c])
                l = l + p
                ps.append(p.astype(BF16))
            l_ref[c] = l
            p_maps.append(jnp.concatenate(ps, axis=1))
        acc_ref[...] += jnp.dot(jnp.concatenate(p_maps, axis=0), v_ref[rows(j), :], preferred_element_type=F32)

    l_ref[...] = jnp.zeros_like(l_ref)
    acc_ref[...] = jnp.zeros_like(acc_ref)
    _for_chunks(i + 1, pass2)
    outs = [acc_ref[c * tq:(c + 1) * tq] / jnp.sum(l_ref[c], axis=-1, keepdims=True) for c in maps]

    lp = lam_ref[...]
    lam = (jnp.exp(jnp.sum(lp[0:1] * lp[1:2], axis=-1, keepdims=True))
           - jnp.exp(jnp.sum(lp[2:3] * lp[3:4], axis=-1, keepdims=True)) + lambda_init)
    o = outs[0] - lam * outs[1]
    o = o * lax.rsqrt(jnp.mean(o * o, axis=-1, keepdims=True) + SUBLN_EPS)
    o_ref[...] = (o * g_ref[...] * (1.0 - lambda_init)).astype(o_ref.dtype)


def diff_attention(proj, col, lam, subln_g, lambda_init, tq=512):
    bsz, s, _ = proj.shape
    hw = 2 * DA_HEAD_DIM
    assert col % hw == 0
    tq = _tile(s, tq)
    qb = col // hw
    kb, vb = qb + DA_HEADS, qb + 2 * DA_HEADS
    return pl.pallas_call(
        functools.partial(_attn_body, tq=tq, lambda_init=lambda_init),
        grid=(bsz, DA_HEADS, s // tq),
        in_specs=[pl.BlockSpec((4, DA_HEAD_DIM), lambda b, h, i: (0, 0)),
                  pl.BlockSpec((1, hw), lambda b, h, i: (0, 0)),
                  pl.BlockSpec((None, tq, hw), lambda b, h, i: (b, i, qb + h)),
                  pl.BlockSpec((None, s, hw), lambda b, h, i: (b, 0, kb + h)),
                  pl.BlockSpec((None, s, hw), lambda b, h, i: (b, 0, vb + h))],
        out_specs=pl.BlockSpec((None, tq, hw), lambda b, h, i: (b, i, h)),
        out_shape=jax.ShapeDtypeStruct((bsz, s, DA_HEADS * hw), BF16),
        scratch_shapes=[pltpu.VMEM((2, s // tq, tq, tq), F32), pltpu.VMEM((2, tq, LANES), F32),
                        pltpu.VMEM((2, tq, LANES), F32), pltpu.VMEM((2 * tq, hw), F32)],
        compiler_params=_params(("parallel", "parallel", "arbitrary")),
        name="diff_attention",
    )(lam, subln_g.reshape(1, hw), proj, proj, proj)


MOE_ROWS = 512
MOE_TOKENS = 256


def _route(ei, n_e):
    n = ei.shape[0]
    e = ei[:, :TOP_K]
    onehot = (e[:, :, None] == jnp.arange(n_e, dtype=jnp.int32)).astype(jnp.int32).reshape(n * TOP_K, n_e)
    incl = jnp.cumsum(onehot, axis=0)
    rank = jnp.sum((incl - onehot) * onehot, axis=1)
    tiles = (incl[-1] + MOE_ROWS - 1) // MOE_ROWS
    tile_end = jnp.cumsum(tiles)
    start = (tile_end - tiles) * MOE_ROWS
    dest = (jnp.sum(onehot * start[None, :], axis=1) + rank).reshape(n, TOP_K)
    max_tiles = (n * TOP_K) // MOE_ROWS + n_e
    n_used = tile_end[-1:]
    t = jnp.minimum(jnp.arange(max_tiles, dtype=jnp.int32), n_used[0] - 1)
    tile_expert = jnp.sum((t[:, None] >= tile_end[None, :]).astype(jnp.int32), axis=1)
    return dest.astype(jnp.int32), tile_expert.astype(jnp.int32), n_used.astype(jnp.int32), max_tiles


def _dispatch_body(dest_ref, hp_ref, init_ref, hs_ref, sem, *, tt):
    del init_ref

    def row_copy(t, k):
        return pltpu.make_async_copy(hp_ref.at[pl.ds(t, 1)], hs_ref.at[pl.ds(dest_ref[0, TOP_K * t + k], 1)], sem)

    def start(t, carry):
        for k in range(TOP_K):
            row_copy(t, k).start()
        return carry

    def wait(t, carry):
        for k in range(TOP_K):
            row_copy(t, k).wait()
        return carry

    lax.fori_loop(0, tt, start, 0)
    lax.fori_loop(0, tt, wait, 0)


def moe_dispatch(hp, dest, n_rows):
    n, w = hp.shape
    tt = _tile(n, MOE_TOKENS, 8)
    return pl.pallas_call(
        functools.partial(_dispatch_body, tt=tt),
        grid=(n // tt,),
        in_specs=[pl.BlockSpec((None, 1, TOP_K * tt), lambda i: (i, 0, 0), memory_space=pltpu.SMEM),
                  pl.BlockSpec((tt, w), lambda i: (i, 0)),
                  pl.BlockSpec(memory_space=pl.ANY)],
        out_specs=pl.BlockSpec(memory_space=pl.ANY),
        out_shape=jax.ShapeDtypeStruct((n_rows, w), U32),
        scratch_shapes=[pltpu.SemaphoreType.DMA(())],
        input_output_aliases={2: 0},
        compiler_params=_params(("arbitrary",)),
        name="moe_dispatch",
    )(dest.reshape(n // tt, 1, TOP_K * tt), hp, jnp.zeros((n_rows, w), U32))


def _zero_unused_tile(nu_ref, o_ref):
    @pl.when(pl.program_id(1) >= nu_ref[0])
    def _():
        o_ref[...] = jnp.zeros_like(o_ref)


def _moe_glu_body(te_ref, nu_ref, a_ref, wg_ref, wu_ref, o_ref):
    del te_ref
    _zero_unused_tile(nu_ref, o_ref)

    @pl.when(pl.program_id(1) < nu_ref[0])
    def _():
        lo, hi = _unpack_halves(a_ref[...])
        half = lo.shape[1]

        def proj(w_ref):
            return (jnp.dot(lo, w_ref[:half].astype(BF16), preferred_element_type=F32)
                    + jnp.dot(hi, w_ref[half:].astype(BF16), preferred_element_type=F32))

        o_ref[...] = _silu_mul(proj(wg_ref), proj(wu_ref)).astype(o_ref.dtype)


def moe_glu(hs, wg, wu, layer, tile_expert, n_used, tn=512):
    rows, half = hs.shape
    f = wg.shape[3]
    tn = _tile(f, tn, MXU_EDGE)
    n_tiles = rows // MOE_ROWS

    def row_tile(r, nu):
        return jnp.minimum(r, nu[0] - 1)

    w_spec = pl.BlockSpec((None, None, 2 * half, tn), lambda j, r, te, nu: (layer, te[r], 0, j))
    return pl.pallas_call(
        _moe_glu_body,
        grid_spec=pltpu.PrefetchScalarGridSpec(
            num_scalar_prefetch=2,
            grid=(f // tn, n_tiles),
            in_specs=[pl.BlockSpec((MOE_ROWS, half), lambda j, r, te, nu: (row_tile(r, nu), 0)), w_spec, w_spec],
            out_specs=pl.BlockSpec((MOE_ROWS, tn), lambda j, r, te, nu: (r, j))),
        out_shape=jax.ShapeDtypeStruct((rows, f), BF16),
        compiler_params=_params(("arbitrary", "arbitrary")),
        name="moe_glu",
    )(tile_expert, n_used, hs, wg, wu)


def _moe_down_body(te_ref, nu_ref, a_ref, w_ref, o_ref):
    del te_ref
    _zero_unused_tile(nu_ref, o_ref)

    @pl.when(pl.program_id(1) < nu_ref[0])
    def _():
        o_ref[...] = jnp.dot(a_ref[...], w_ref[...].astype(BF16), preferred_element_type=F32)


def moe_down(act, wd, layer, tile_expert, n_used, tn=1024):
    rows, f = act.shape
    d = wd.shape[3]
    tn = _tile(d, tn)
    n_tiles = rows // MOE_ROWS

    def row_tile(r, nu):
        return jnp.minimum(r, nu[0] - 1)

    return pl.pallas_call(
        _moe_down_body,
        grid_spec=pltpu.PrefetchScalarGridSpec(
            num_scalar_prefetch=2,
            grid=(d // tn, n_tiles),
            in_specs=[pl.BlockSpec((MOE_ROWS, f), lambda j, r, te, nu: (row_tile(r, nu), 0)),
                      pl.BlockSpec((None, None, f, tn), lambda j, r, te, nu: (layer, te[r], 0, j))],
            out_specs=pl.BlockSpec((MOE_ROWS, tn), lambda j, r, te, nu: (r, j))),
        out_shape=jax.ShapeDtypeStruct((rows, d), F32),
        compiler_params=_params(("arbitrary", "arbitrary")),
        name="moe_down",
    )(tile_expert, n_used, act, wd)


def _combine_body(dest_ref, next_ref, ew_ref, x_ref, *rest, tt, has_norm):
    if has_norm:
        g_ref, ys_ref, o_ref, ybuf, sem = rest
    else:
        ys_ref, o_ref, ybuf, sem = rest
    i, n_steps = pl.program_id(0), pl.num_programs(0)
    slot = i % 2

    def row_copy(d_ref, sl, t, k):
        return pltpu.make_async_copy(ys_ref.at[pl.ds(d_ref[0, TOP_K * t + k], 1)],
                                     ybuf.at[sl, k, pl.ds(t, 1)], sem.at[sl])

    def start_all(d_ref, sl):
        def start(t, carry):
            for k in range(TOP_K):
                row_copy(d_ref, sl, t, k).start()
            return carry

        lax.fori_loop(0, tt, start, 0)

    @pl.when(i == 0)
    def _():
        start_all(dest_ref, slot)

    @pl.when(i + 1 < n_steps)
    def _():
        start_all(next_ref, 1 - slot)

    def wait(t, carry):
        for k in range(TOP_K):
            row_copy(dest_ref, slot, t, k).wait()
        return carry

    lax.fori_loop(0, tt, wait, 0)
    ew = ew_ref[...]
    y = x_ref[...] + ew[:, 0:1] * ybuf[slot, 0] + ew[:, 1:2] * ybuf[slot, 1]
    o_ref[...] = _rms(y, g_ref[...]) if has_norm else y


def moe_combine(x, ys, dest, ew, norm_g=None):
    n, d = x.shape
    tt = _tile(n, MOE_TOKENS, 8)
    n_steps = n // tt
    dest = dest.reshape(n_steps, 1, TOP_K * tt)

    def d_spec(ahead):
        return pl.BlockSpec((None, 1, TOP_K * tt), lambda i: (jnp.minimum(i + ahead, n_steps - 1), 0, 0),
                            memory_space=pltpu.SMEM)

    in_specs = [d_spec(0), d_spec(1),
                pl.BlockSpec((tt, LANES), lambda i: (i, 0)),
                pl.BlockSpec((tt, d), lambda i: (i, 0))]
    args = [dest, dest, ew, x]
    if norm_g is not None:
        in_specs.append(pl.BlockSpec((1, d), lambda i: (0, 0)))
        args.append(norm_g.reshape(1, d))
    return pl.pallas_call(
        functools.partial(_combine_body, tt=tt, has_norm=norm_g is not None),
        grid=(n_steps,),
        in_specs=in_specs + [pl.BlockSpec(memory_space=pl.ANY)],
        out_specs=pl.BlockSpec((tt, d), lambda i: (i, 0)),
        out_shape=jax.ShapeDtypeStruct((n, d), F32),
        scratch_shapes=[pltpu.VMEM((2, TOP_K, tt, d), F32), pltpu.SemaphoreType.DMA((2,))],
        compiler_params=_params(("arbitrary",)),
        name="moe_combine",
    )(*args, ys)


def moe_ffn(x, g, w_router, wg, wu, wd, layer, norm_g=None):
    n_e = wg.shape[1]
    hp, ei, ew = rmsnorm_router(x, g, w_router)
    dest, tile_expert, n_used, max_tiles = _route(ei, n_e)
    hs = moe_dispatch(hp, dest, max_tiles * MOE_ROWS)
    act = moe_glu(hs, wg, wu, layer, tile_expert, n_used)
    ys = moe_down(act, wd, layer, tile_expert, n_used)
    return moe_combine(x, ys, dest, ew, norm_g)


def _mixer(x, h, l, w_in, conv_w, w_a_out, pool_w, pool_scale, w_p_out, lam, subln, w_c_out, w_o,
           lambda_init, bsz, s):
    n, d = x.shape
    cw, pw, aw = conv_w.shape[2], pool_scale.shape[1], w_c_out.shape[1]
    proj = matmul(h, w_in, l, BF16, resident_a=True)
    proj3 = proj.reshape(bsz, s, proj.shape[1])
    ya = short_conv(proj3, conv_w[l]).reshape(n, cw)
    yp = pool_mix(proj3, 3 * cw, pool_w[l], pool_scale[l]).reshape(n, pw)
    yc = diff_attention(proj3, 3 * cw + pw, lam[l], subln[l], lambda_init).reshape(n, aw)
    merged = merge(ya, yp, yc, w_a_out, w_p_out, w_c_out, l, proj, 3 * cw + pw + 3 * aw)
    return matmul(merged, w_o, l, F32, res=x, tn=256)


def kernel(x, norm_mix, w_in, conv_w, w_a_out, pool_w, pool_scale, w_p_out, lam, subln, w_c_out, w_o,
           norm_ffn, ffn_w_gate, ffn_w_up, ffn_w_down, w_router, moe_w_gate, moe_w_up, moe_w_down,
           norm_final):
    bsz, s, d = x.shape
    depth = norm_mix.shape[0]
    n = bsz * s
    x = x.reshape(n, d)
    for l in range(depth):
        lambda_init = 0.8 - 0.6 * math.exp(-0.3 * l)
        h = rmsnorm(x, norm_mix[l], BF16)
        x = _mixer(x, h, l, w_in, conv_w, w_a_out, pool_w, pool_scale, w_p_out, lam, subln, w_c_out, w_o,
                   lambda_init, bsz, s)
        j = l // 2
        if l % 2 == 0:
            h = rmsnorm(x, norm_ffn[l], BF16)
            act = glu_in(h, ffn_w_gate, ffn_w_up, j)
            x = matmul_acc(act, ffn_w_down, j, x)
        elif l < depth - 1:
            x = moe_ffn(x, norm_ffn[l], w_router[j], moe_w_gate, moe_w_up, moe_w_down, j)
        else:
            return moe_ffn(x, norm_ffn[l], w_router[j], moe_w_gate, moe_w_up, moe_w_down, j,
                           norm_g=norm_final).reshape(bsz, s, d)
    return rmsnorm(x, norm_final, F32).reshape(bsz, s, d)
```

```python
import functools
import math

import jax
import jax.numpy as jnp
from jax import lax
from jax.experimental import pallas as pl
from jax.experimental.pallas import tpu as pltpu

F32 = jnp.float32
BF16 = jnp.bfloat16
U32 = jnp.uint32

NORM_EPS = 1e-6
SUBLN_EPS = 1e-5
CONV_K = 3
POOL_WINDOWS = (2, 4, 8, 16)
DA_HEADS = 8
DA_HEAD_DIM = 128
N_BRANCHES = 3
TOP_K = 2

LANES = 128
BF16_ROWS = 16
MXU_EDGE = 256
VMEM_BYTES = 64 * 1024 * 1024
VMEM_LIMIT = VMEM_BYTES - 8 * 1024 * 1024

HALO = BF16_ROWS
assert HALO >= max(POOL_WINDOWS) and HALO >= CONV_K
LOG2E = math.log2(math.e)
HI16 = 0xFFFF0000


def _tile(dim, pref, unit=LANES):
    if dim <= pref:
        return dim
    t = (pref // unit) * unit
    while t > unit and dim % t:
        t -= unit
    assert dim % t == 0, (dim, pref, unit)
    return t


def _params(sem):
    return pltpu.CompilerParams(dimension_semantics=sem, vmem_limit_bytes=VMEM_LIMIT)


def _resident(block_shape, index_map):
    return pl.BlockSpec(block_shape, index_map, pipeline_mode=pl.Buffered(1))


def _rms(x, g):
    ms = jnp.mean(x * x, axis=-1, keepdims=True)
    return x * lax.rsqrt(ms + NORM_EPS) * g


def _rmsnorm_body(x_ref, g_ref, o_ref):
    o_ref[...] = _rms(x_ref[...], g_ref[...]).astype(o_ref.dtype)


def rmsnorm(x, g, out_dtype):
    n, d = x.shape
    tr = _tile(n, 256, 8)
    return pl.pallas_call(
        _rmsnorm_body,
        grid=(n // tr,),
        in_specs=[pl.BlockSpec((tr, d), lambda i: (i, 0)),
                  pl.BlockSpec((1, d), lambda i: (0, 0))],
        out_specs=pl.BlockSpec((tr, d), lambda i: (i, 0)),
        out_shape=jax.ShapeDtypeStruct((n, d), out_dtype),
        compiler_params=_params(("parallel",)),
        name="rmsnorm",
    )(x, g.reshape(1, d))


def _pack_halves(h):
    half = h.shape[1] // 2
    lo = lax.bitcast_convert_type(h[:, :half].astype(BF16).astype(F32), U32)
    hi = lax.bitcast_convert_type(h[:, half:].astype(BF16).astype(F32), U32)
    return (lo >> 16) | (hi & jnp.uint32(HI16))


def _unpack_halves(p):
    lo = lax.bitcast_convert_type(p << 16, F32).astype(BF16)
    hi = lax.bitcast_convert_type(p & jnp.uint32(HI16), F32).astype(BF16)
    return lo, hi


def _split_bf16(x):
    hi = x.astype(BF16)
    return hi, (x - hi.astype(F32)).astype(BF16)


def _rmsnorm_router_body(x_ref, g_ref, wr_ref, hp_ref, ei_ref, ew_ref, *, n_experts):
    h = _rms(x_ref[...], g_ref[...])
    hp_ref[...] = _pack_halves(h)
    h_hi, h_lo = _split_bf16(h)
    w_hi, w_lo = _split_bf16(wr_ref[...])
    logits = (jnp.dot(h_hi, w_hi, preferred_element_type=F32) + jnp.dot(h_hi, w_lo, preferred_element_type=F32)
              + jnp.dot(h_lo, w_hi, preferred_element_type=F32))
    lane = lax.broadcasted_iota(jnp.int32, logits.shape, 1)
    neg = jnp.float32(-jnp.inf)
    logits = jnp.where(lane < n_experts, logits, neg)
    m1 = jnp.max(logits, axis=-1, keepdims=True)
    i1 = jnp.min(jnp.where(logits == m1, lane, LANES), axis=-1, keepdims=True)
    rest = jnp.where(lane == i1, neg, logits)
    m2 = jnp.max(rest, axis=-1, keepdims=True)
    i2 = jnp.min(jnp.where(rest == m2, lane, LANES), axis=-1, keepdims=True)
    e2 = jnp.exp(m2 - m1)
    w1 = 1.0 / (1.0 + e2)
    w2 = e2 / (1.0 + e2)
    ei_ref[...] = jnp.where(lane == 0, i1, jnp.where(lane == 1, i2, 0))
    ew_ref[...] = jnp.where(lane == 0, w1, jnp.where(lane == 1, w2, 0.0))


def rmsnorm_router(x, g, w_router):
    n, d = x.shape
    n_experts = w_router.shape[1]
    wr = jnp.pad(w_router, ((0, 0), (0, LANES - n_experts)))
    tr = _tile(n, 256, 8)
    return pl.pallas_call(
        functools.partial(_rmsnorm_router_body, n_experts=n_experts),
        grid=(n // tr,),
        in_specs=[pl.BlockSpec((tr, d), lambda i: (i, 0)),
                  pl.BlockSpec((1, d), lambda i: (0, 0)),
                  pl.BlockSpec((d, LANES), lambda i: (0, 0))],
        out_specs=[pl.BlockSpec((tr, d // 2), lambda i: (i, 0)),
                   pl.BlockSpec((tr, LANES), lambda i: (i, 0)),
                   pl.BlockSpec((tr, LANES), lambda i: (i, 0))],
        out_shape=[jax.ShapeDtypeStruct((n, d // 2), U32),
                   jax.ShapeDtypeStruct((n, LANES), jnp.int32),
                   jax.ShapeDtypeStruct((n, LANES), F32)],
        compiler_params=_params(("parallel",)),
        name="rmsnorm_router",
    )(x, g.reshape(1, d), wr)


ACC_ROWS = 512


def _mm_body(a_ref, w_ref, *rest, has_res):
    o_ref = rest[-1]
    w = w_ref[...].astype(BF16)
    rc = min(ACC_ROWS, o_ref.shape[0])
    for c in range(o_ref.shape[0] // rc):
        rows = slice(c * rc, (c + 1) * rc)
        r = jnp.dot(a_ref[rows], w, preferred_element_type=F32)
        if has_res:
            r = r + rest[0][rows]
        o_ref[rows] = r.astype(o_ref.dtype)


def matmul(a, w, layer, out_dtype, res=None, tm=2048, tn=512, resident_a=False):
    m, kd = a.shape
    n = w.shape[2]
    tm, tn = _tile(m, tm), _tile(n, tn)
    a_spec = _resident if resident_a else pl.BlockSpec
    in_specs = [a_spec((tm, kd), lambda i, j: (i, 0)),
                pl.BlockSpec((None, kd, tn), lambda i, j: (layer, 0, j))]
    args = [a, w]
    if res is not None:
        in_specs.append(pl.BlockSpec((tm, tn), lambda i, j: (i, j)))
        args.append(res)
    return pl.pallas_call(
        functools.partial(_mm_body, has_res=res is not None),
        grid=(m // tm, n // tn),
        in_specs=in_specs,
        out_specs=pl.BlockSpec((tm, tn), lambda i, j: (i, j)),
        out_shape=jax.ShapeDtypeStruct((m, n), out_dtype),
        compiler_params=_params(("parallel", "arbitrary")),
        name="matmul_res" if res is not None else "matmul",
    )(*args)


def _mm_acc_body(a_ref, w_ref, res_ref, o_ref):
    @pl.when(pl.program_id(2) == 0)
    def _():
        o_ref[...] = res_ref[...]

    w = w_ref[...].astype(BF16)
    rc = min(ACC_ROWS, o_ref.shape[0])
    for r in range(o_ref.shape[0] // rc):
        rows = slice(r * rc, (r + 1) * rc)
        o_ref[rows] += jnp.dot(a_ref[rows], w, preferred_element_type=F32)


def matmul_acc(a, w, layer, res, tm=2048, tn=1024, tk=1024):
    m, kd = a.shape
    n = w.shape[2]
    tm, tn, tk = _tile(m, tm), _tile(n, tn, MXU_EDGE), _tile(kd, tk, MXU_EDGE)
    return pl.pallas_call(
        _mm_acc_body,
        grid=(m // tm, n // tn, kd // tk),
        in_specs=[pl.BlockSpec((tm, tk), lambda i, j, k: (i, k)),
                  pl.BlockSpec((None, tk, tn), lambda i, j, k: (layer, k, j)),
                  pl.BlockSpec((tm, tn), lambda i, j, k: (i, j))],
        out_specs=pl.BlockSpec((tm, tn), lambda i, j, k: (i, j)),
        out_shape=jax.ShapeDtypeStruct((m, n), F32),
        compiler_params=_params(("parallel", "parallel", "arbitrary")),
        name="matmul_acc",
    )(a, w, res)


def _silu_mul(g, u):
    return g * jax.nn.sigmoid(g) * u


def _glu_body(a_ref, wg_ref, wu_ref, o_ref):
    wg, wu = wg_ref[...].astype(BF16), wu_ref[...].astype(BF16)
    rc = min(ACC_ROWS, o_ref.shape[0])
    for c in range(o_ref.shape[0] // rc):
        rows = slice(c * rc, (c + 1) * rc)
        a = a_ref[rows]
        g = jnp.dot(a, wg, preferred_element_type=F32)
        u = jnp.dot(a, wu, preferred_element_type=F32)
        o_ref[rows] = _silu_mul(g, u).astype(o_ref.dtype)


def glu_in(a, wg, wu, layer, tm=2048, tn=256):
    m, kd = a.shape
    f = wg.shape[2]
    tm, tn = _tile(m, tm), _tile(f, tn, MXU_EDGE)
    w_spec = pl.BlockSpec((None, kd, tn), lambda i, j: (layer, 0, j))
    return pl.pallas_call(
        _glu_body,
        grid=(m // tm, f // tn),
        in_specs=[pl.BlockSpec((tm, kd), lambda i, j: (i, 0)), w_spec, w_spec],
        out_specs=pl.BlockSpec((tm, tn), lambda i, j: (i, j)),
        out_shape=jax.ShapeDtypeStruct((m, f), BF16),
        compiler_params=_params(("parallel", "arbitrary")),
        name="glu_in",
    )(a, wg, wu)


MERGE_ROWS = 256


def _merge_body(a_ref, p_ref, c_ref, wa_ref, wp_ref, wc_ref, g0_ref, g1_ref, g2_ref, o_ref):
    branches = ((a_ref, wa_ref[...].astype(BF16), g0_ref),
                (p_ref, wp_ref[...].astype(BF16), g1_ref),
                (c_ref, wc_ref[...].astype(BF16), g2_ref))
    rc = min(MERGE_ROWS, o_ref.shape[0])
    for r in range(o_ref.shape[0] // rc):
        rows = slice(r * rc, (r + 1) * rc)
        acc = None
        for x_ref, w, g_ref in branches:
            y = jax.nn.sigmoid(g_ref[rows].astype(F32)) * jnp.dot(x_ref[rows], w, preferred_element_type=F32)
            acc = y if acc is None else acc + y
        o_ref[rows] = acc.astype(o_ref.dtype)


def merge(ya, yp, yc, wa, wp, wc, layer, proj, gate_col, tm=2048, tn=256):
    m = ya.shape[0]
    d = wa.shape[2]
    tm, tn = _tile(m, tm), _tile(d, tn)
    assert gate_col % tn == 0
    gb, nd = gate_col // tn, d // tn

    def x_spec(x):
        return pl.BlockSpec((tm, x.shape[1]), lambda i, j: (i, 0))

    def w_spec(w):
        return pl.BlockSpec((None, w.shape[1], tn), lambda i, j: (layer, 0, j))

    def g_spec(b):
        return pl.BlockSpec((tm, tn), lambda i, j: (i, gb + b * nd + j))

    return pl.pallas_call(
        _merge_body,
        grid=(m // tm, nd),
        in_specs=[x_spec(ya), x_spec(yp), x_spec(yc), w_spec(wa), w_spec(wp), w_spec(wc),
                  g_spec(0), g_spec(1), g_spec(2)],
        out_specs=pl.BlockSpec((tm, tn), lambda i, j: (i, j)),
        out_shape=jax.ShapeDtypeStruct((m, d), BF16),
        compiler_params=_params(("parallel", "parallel")),
        name="merge",
    )(ya, yp, yc, wa, wp, wc, proj, proj, proj)


def _with_halo(prev, cur, first):
    prev = jnp.where(first, 0.0, prev)
    return jnp.concatenate([prev, cur], axis=0)


def _conv_body(u_ref, b_ref, c_ref, up_ref, cp_ref, w_ref, o_ref):
    first = pl.program_id(1) == 0
    z = c_ref[...].astype(F32) * u_ref[...].astype(F32)
    zp = cp_ref[...].astype(F32) * up_ref[...].astype(F32)
    ze = _with_halo(zp, z, first)
    z1 = pltpu.roll(ze, 1, 0)[HALO:]
    z2 = pltpu.roll(ze, 2, 0)[HALO:]
    w = w_ref[...]
    zc = w[0:1] * z2 + w[1:2] * z1 + w[2:3] * z
    o_ref[...] = (b_ref[...].astype(F32) * zc).astype(o_ref.dtype)


def short_conv(proj, conv_w, ts=2048, cw=256):
    bsz, s, _ = proj.shape
    width = conv_w.shape[1]
    ts, cw = _tile(s, ts, HALO), _tile(width, cw)
    nc, hb = width // cw, ts // HALO

    def cur(off):
        return pl.BlockSpec((None, ts, cw), lambda b, i, j: (b, i, off * nc + j))

    def prev(off):
        return pl.BlockSpec((None, HALO, cw), lambda b, i, j: (b, jnp.maximum(i * hb - 1, 0), off * nc + j))

    return pl.pallas_call(
        _conv_body,
        grid=(bsz, s // ts, nc),
        in_specs=[cur(0), cur(1), cur(2), prev(0), prev(2),
                  pl.BlockSpec((CONV_K, cw), lambda b, i, j: (0, j))],
        out_specs=pl.BlockSpec((None, ts, cw), lambda b, i, j: (b, i, j)),
        out_shape=jax.ShapeDtypeStruct((bsz, s, width), BF16),
        compiler_params=_params(("parallel", "parallel", "parallel")),
        name="short_conv",
    )(proj, proj, proj, proj, proj, conv_w)


def _pool_body(u_ref, up_ref, pw_ref, ps_ref, o_ref, *, ts):
    i, g = pl.program_id(1), pl.program_id(2)
    u = u_ref[...].astype(F32)
    ue = _with_halo(up_ref[...].astype(F32), u, i == 0)
    t1 = i * ts + lax.broadcasted_iota(jnp.int32, u.shape, 0) + 1
    for gi, w in enumerate(POOL_WINDOWS):
        assert w & (w - 1) == 0 and w <= HALO

        @pl.when(g == gi)
        def _(w=w):
            s, span = ue, 1
            while span < w:
                s = s + pltpu.roll(s, span, 0)
                span *= 2
            mean = s[HALO:] / jnp.minimum(t1, w).astype(F32)
            mixed = (mean - u).astype(BF16)
            y = jnp.dot(mixed, pw_ref[...].astype(BF16), preferred_element_type=F32)
            o_ref[...] = (y * ps_ref[...]).astype(o_ref.dtype)


def pool_mix(proj, col, pool_w, pool_scale, ts=2048):
    bsz, s, _ = proj.shape
    n_g, gd, _ = pool_w.shape
    assert n_g == len(POOL_WINDOWS) and col % gd == 0
    ts = _tile(s, ts, HALO)
    cb, hb = col // gd, ts // HALO
    return pl.pallas_call(
        functools.partial(_pool_body, ts=ts),
        grid=(bsz, s // ts, n_g),
        in_specs=[pl.BlockSpec((None, ts, gd), lambda b, i, g: (b, i, cb + g)),
                  pl.BlockSpec((None, HALO, gd), lambda b, i, g: (b, jnp.maximum(i * hb - 1, 0), cb + g)),
                  pl.BlockSpec((None, gd, gd), lambda b, i, g: (g, 0, 0)),
                  pl.BlockSpec((1, gd), lambda b, i, g: (0, g))],
        out_specs=pl.BlockSpec((None, ts, gd), lambda b, i, g: (b, i, g)),
        out_shape=jax.ShapeDtypeStruct((bsz, s, n_g * gd), BF16),
        compiler_params=_params(("parallel", "parallel", "parallel")),
        name="pool_mix",
    )(proj, proj, pool_w, pool_scale.reshape(1, n_g * gd))


CHUNK_GROUP = 2


def _for_chunks(n, fn):
    def group(jj, carry):
        for u in range(CHUNK_GROUP):
            fn(CHUNK_GROUP * jj + u)
        return carry

    def single(j, carry):
        fn(j)
        return carry

    lax.fori_loop(0, n // CHUNK_GROUP, group, 0)
    lax.fori_loop(n - n % CHUNK_GROUP, n, single, 0)


def _attn_body(lam_ref, g_ref, q_ref, k_ref, v_ref, o_ref, s_ref, m_ref, l_ref, acc_ref, *, tq, lambda_init):
    dh = DA_HEAD_DIM
    i = pl.program_id(2)
    maps = range(2)
    slabs = [slice(b * LANES, (b + 1) * LANES) for b in range(tq // LANES)]
    q = (q_ref[...].astype(F32) * (dh ** -0.5 * LOG2E)).astype(BF16)

    def rows(j):
        return pl.ds(pl.multiple_of(j * tq, tq), tq)

    def scores(j, c):
        cols = slice(c * dh, (c + 1) * dh)
        return lax.dot_general(q[:, cols], k_ref[rows(j), cols], (((1,), (1,)), ((), ())),
                               preferred_element_type=F32)

    def keep(j, c, s):
        s_ref[c, j] = s
        m = m_ref[c]
        for sl in slabs:
            m = jnp.maximum(m, s[:, sl])
        m_ref[c] = m

    def pass1(j):
        for c in maps:
            keep(j, c, scores(j, c))

    m_ref[...] = jnp.full_like(m_ref, -jnp.inf)
    _for_chunks(i, pass1)
    row = lax.broadcasted_iota(jnp.int32, (tq, tq), 0)
    col = lax.broadcasted_iota(jnp.int32, (tq, tq), 1)
    for c in maps:
        keep(i, c, jnp.where(col <= row, scores(i, c), -jnp.inf))
    mb = [jnp.broadcast_to(jnp.max(m_ref[c], axis=-1, keepdims=True), (tq, LANES)) for c in maps]

    def pass2(j):
        p_maps = []
        for c in maps:
            s = s_ref[c, j]
            l = l_ref[c]
            ps = []
            for sl in slabs:
                p = jnp.exp2(s[:, sl] - mb[c])
                l = l + p
                ps.append(p.astype(BF16))
            l_ref[c] = l
            p_maps.append(jnp.concatenate(ps, axis=1))
        acc_ref[...] += jnp.dot(jnp.concatenate(p_maps, axis=0), v_ref[rows(j), :], preferred_element_type=F32)

    l_ref[...] = jnp.zeros_like(l_ref)
    acc_ref[...] = jnp.zeros_like(acc_ref)
    _for_chunks(i + 1, pass2)
    outs = [acc_ref[c * tq:(c + 1) * tq] / jnp.sum(l_ref[c], axis=-1, keepdims=True) for c in maps]

    lp = lam_ref[...]
    lam = (jnp.exp(jnp.sum(lp[0:1] * lp[1:2], axis=-1, keepdims=True))
           - jnp.exp(jnp.sum(lp[2:3] * lp[3:4], axis=-1, keepdims=True)) + lambda_init)
    o = outs[0] - lam * outs[1]
    o = o * lax.rsqrt(jnp.mean(o * o, axis=-1, keepdims=True) + SUBLN_EPS)
    o_ref[...] = (o * g_ref[...] * (1.0 - lambda_init)).astype(o_ref.dtype)


def diff_attention(proj, col, lam, subln_g, lambda_init, tq=512):
    bsz, s, _ = proj.shape
    hw = 2 * DA_HEAD_DIM
    assert col % hw == 0
    tq = _tile(s, tq)
    qb = col // hw
    kb, vb = qb + DA_HEADS, qb + 2 * DA_HEADS
    return pl.pallas_call(
        functools.partial(_attn_body, tq=tq, lambda_init=lambda_init),
        grid=(bsz, DA_HEADS, s // tq),
        in_specs=[pl.BlockSpec((4, DA_HEAD_DIM), lambda b, h, i: (0, 0)),
                  pl.BlockSpec((1, hw), lambda b, h, i: (0, 0)),
                  pl.BlockSpec((None, tq, hw), lambda b, h, i: (b, i, qb + h)),
                  pl.BlockSpec((None, s, hw), lambda b, h, i: (b, 0, kb + h)),
                  pl.BlockSpec((None, s, hw), lambda b, h, i: (b, 0, vb + h))],
        out_specs=pl.BlockSpec((None, tq, hw), lambda b, h, i: (b, i, h)),
        out_shape=jax.ShapeDtypeStruct((bsz, s, DA_HEADS * hw), BF16),
        scratch_shapes=[pltpu.VMEM((2, s // tq, tq, tq), F32), pltpu.VMEM((2, tq, LANES), F32),
                        pltpu.VMEM((2, tq, LANES), F32), pltpu.VMEM((2 * tq, hw), F32)],
        compiler_params=_params(("parallel", "parallel", "arbitrary")),
        name="diff_attention",
    )(lam, subln_g.reshape(1, hw), proj, proj, proj)


MOE_ROWS = 512
MOE_TOKENS = 256


def _route(ei, n_e):
    n = ei.shape[0]
    e = ei[:, :TOP_K]
    onehot = (e[:, :, None] == jnp.arange(n_e, dtype=jnp.int32)).astype(jnp.int32).reshape(n * TOP_K, n_e)
    incl = jnp.cumsum(onehot, axis=0)
    rank = jnp.sum((incl - onehot) * onehot, axis=1)
    tiles = (incl[-1] + MOE_ROWS - 1) // MOE_ROWS
    tile_end = jnp.cumsum(tiles)
    start = (tile_end - tiles) * MOE_ROWS
    dest = (jnp.sum(onehot * start[None, :], axis=1) + rank).reshape(n, TOP_K)
    max_tiles = (n * TOP_K) // MOE_ROWS + n_e
    n_used = tile_end[-1:]
    t = jnp.minimum(jnp.arange(max_tiles, dtype=jnp.int32), n_used[0] - 1)
    tile_expert = jnp.sum((t[:, None] >= tile_end[None, :]).astype(jnp.int32), axis=1)
    return dest.astype(jnp.int32), tile_expert.astype(jnp.int32), n_used.astype(jnp.int32), max_tiles


def _dispatch_body(dest_ref, hp_ref, init_ref, hs_ref, sem, *, tt):
    del init_ref

    def row_copy(t, k):
        return pltpu.make_async_copy(hp_ref.at[pl.ds(t, 1)], hs_ref.at[pl.ds(dest_ref[0, TOP_K * t + k], 1)], sem)

    def start(t, carry):
        for k in range(TOP_K):
            row_copy(t, k).start()
        return carry

    lax.fori_loop(0, tt, start, 0, unroll=4)
    all_rows = hs_ref.at[pl.ds(0, TOP_K * tt)]
    pltpu.make_async_copy(all_rows, all_rows, sem).wait()


def moe_dispatch(hp, dest, n_rows):
    n, w = hp.shape
    tt = _tile(n, MOE_TOKENS, 8)
    return pl.pallas_call(
        functools.partial(_dispatch_body, tt=tt),
        grid=(n // tt,),
        in_specs=[pl.BlockSpec((None, 1, TOP_K * tt), lambda i: (i, 0, 0), memory_space=pltpu.SMEM),
                  pl.BlockSpec((tt, w), lambda i: (i, 0)),
                  pl.BlockSpec(memory_space=pl.ANY)],
        out_specs=pl.BlockSpec(memory_space=pl.ANY),
        out_shape=jax.ShapeDtypeStruct((n_rows, w), U32),
        scratch_shapes=[pltpu.SemaphoreType.DMA(())],
        input_output_aliases={2: 0},
        compiler_params=_params(("arbitrary",)),
        name="moe_dispatch",
    )(dest.reshape(n // tt, 1, TOP_K * tt), hp, jnp.zeros((n_rows, w), U32))


def _zero_unused_tile(nu_ref, o_ref):
    @pl.when(pl.program_id(1) >= nu_ref[0])
    def _():
        o_ref[...] = jnp.zeros_like(o_ref)


def _moe_glu_body(te_ref, nu_ref, a_ref, wg_ref, wu_ref, o_ref):
    del te_ref
    _zero_unused_tile(nu_ref, o_ref)

    @pl.when(pl.program_id(1) < nu_ref[0])
    def _():
        lo, hi = _unpack_halves(a_ref[...])
        half = lo.shape[1]

        def proj(w_ref):
            return (jnp.dot(lo, w_ref[:half].astype(BF16), preferred_element_type=F32)
                    + jnp.dot(hi, w_ref[half:].astype(BF16), preferred_element_type=F32))

        o_ref[...] = _silu_mul(proj(wg_ref), proj(wu_ref)).astype(o_ref.dtype)


def moe_glu(hs, wg, wu, layer, tile_expert, n_used, tn=512):
    rows, half = hs.shape
    f = wg.shape[3]
    tn = _tile(f, tn, MXU_EDGE)
    n_tiles = rows // MOE_ROWS

    def row_tile(r, nu):
        return jnp.minimum(r, nu[0] - 1)

    w_spec = pl.BlockSpec((None, None, 2 * half, tn), lambda j, r, te, nu: (layer, te[r], 0, j))
    return pl.pallas_call(
        _moe_glu_body,
        grid_spec=pltpu.PrefetchScalarGridSpec(
            num_scalar_prefetch=2,
            grid=(f // tn, n_tiles),
            in_specs=[pl.BlockSpec((MOE_ROWS, half), lambda j, r, te, nu: (row_tile(r, nu), 0)), w_spec, w_spec],
            out_specs=pl.BlockSpec((MOE_ROWS, tn), lambda j, r, te, nu: (r, j))),
        out_shape=jax.ShapeDtypeStruct((rows, f), BF16),
        compiler_params=_params(("arbitrary", "arbitrary")),
        name="moe_glu",
    )(tile_expert, n_used, hs, wg, wu)


def _moe_down_body(te_ref, nu_ref, a_ref, w_ref, o_ref):
    del te_ref
    _zero_unused_tile(nu_ref, o_ref)

    @pl.when(pl.program_id(1) < nu_ref[0])
    def _():
        o_ref[...] = jnp.dot(a_ref[...], w_ref[...].astype(BF16), preferred_element_type=F32)


def moe_down(act, wd, layer, tile_expert, n_used, tn=1024):
    rows, f = act.shape
    d = wd.shape[3]
    tn = _tile(d, tn)
    n_tiles = rows // MOE_ROWS

    def row_tile(r, nu):
        return jnp.minimum(r, nu[0] - 1)

    return pl.pallas_call(
        _moe_down_body,
        grid_spec=pltpu.PrefetchScalarGridSpec(
            num_scalar_prefetch=2,
            grid=(d // tn, n_tiles),
            in_specs=[pl.BlockSpec((MOE_ROWS, f), lambda j, r, te, nu: (row_tile(r, nu), 0)),
                      pl.BlockSpec((None, None, f, tn), lambda j, r, te, nu: (layer, te[r], 0, j))],
            out_specs=pl.BlockSpec((MOE_ROWS, tn), lambda j, r, te, nu: (r, j))),
        out_shape=jax.ShapeDtypeStruct((rows, d), F32),
        compiler_params=_params(("arbitrary", "arbitrary")),
        name="moe_down",
    )(tile_expert, n_used, act, wd)


def _combine_body(dest_ref, next_ref, ew_ref, x_ref, *rest, tt, has_norm):
    if has_norm:
        g_ref, ys_ref, o_ref, ybuf, sem = rest
    else:
        ys_ref, o_ref, ybuf, sem = rest
    i, n_steps = pl.program_id(0), pl.num_programs(0)
    slot = i % 2

    def row_copy(d_ref, sl, t, k):
        return pltpu.make_async_copy(ys_ref.at[pl.ds(d_ref[0, TOP_K * t + k], 1)],
                                     ybuf.at[sl, k, pl.ds(t, 1)], sem.at[sl])

    def start_all(d_ref, sl):
        def start(t, carry):
            for k in range(TOP_K):
                row_copy(d_ref, sl, t, k).start()
            return carry

        lax.fori_loop(0, tt, start, 0, unroll=4)

    @pl.when(i == 0)
    def _():
        start_all(dest_ref, slot)

    @pl.when(i + 1 < n_steps)
    def _():
        start_all(next_ref, 1 - slot)

    pltpu.make_async_copy(ybuf.at[slot], ybuf.at[slot], sem.at[slot]).wait()
    ew = ew_ref[...]
    y = x_ref[...] + ew[:, 0:1] * ybuf[slot, 0] + ew[:, 1:2] * ybuf[slot, 1]
    o_ref[...] = _rms(y, g_ref[...]) if has_norm else y


def moe_combine(x, ys, dest, ew, norm_g=None):
    n, d = x.shape
    tt = _tile(n, MOE_TOKENS, 8)
    n_steps = n // tt
    dest = dest.reshape(n_steps, 1, TOP_K * tt)

    def d_spec(ahead):
        return pl.BlockSpec((None, 1, TOP_K * tt), lambda i: (jnp.minimum(i + ahead, n_steps - 1), 0, 0),
                            memory_space=pltpu.SMEM)

    in_specs = [d_spec(0), d_spec(1),
                pl.BlockSpec((tt, LANES), lambda i: (i, 0)),
                pl.BlockSpec((tt, d), lambda i: (i, 0))]
    args = [dest, dest, ew, x]
    if norm_g is not None:
        in_specs.append(pl.BlockSpec((1, d), lambda i: (0, 0)))
        args.append(norm_g.reshape(1, d))
    return pl.pallas_call(
        functools.partial(_combine_body, tt=tt, has_norm=norm_g is not None),
        grid=(n_steps,),
        in_specs=in_specs + [pl.BlockSpec(memory_space=pl.ANY)],
        out_specs=pl.BlockSpec((tt, d), lambda i: (i, 0)),
        out_shape=jax.ShapeDtypeStruct((n, d), F32),
        scratch_shapes=[pltpu.VMEM((2, TOP_K, tt, d), F32), pltpu.SemaphoreType.DMA((2,))],
        compiler_params=_params(("arbitrary",)),
        name="moe_combine",
    )(*args, ys)


def moe_ffn(x, g, w_router, wg, wu, wd, layer, norm_g=None):
    n_e = wg.shape[1]
    hp, ei, ew = rmsnorm_router(x, g, w_router)
    dest, tile_expert, n_used, max_tiles = _route(ei, n_e)
    hs = moe_dispatch(hp, dest, max_tiles * MOE_ROWS)
    act = moe_glu(hs, wg, wu, layer, tile_expert, n_used)
    ys = moe_down(act, wd, layer, tile_expert, n_used)
    return moe_combine(x, ys, dest, ew, norm_g)


def _mixer(x, h, l, w_in, conv_w, w_a_out, pool_w, pool_scale, w_p_out, lam, subln, w_c_out, w_o,
           lambda_init, bsz, s):
    n, d = x.shape
    cw, pw, aw = conv_w.shape[2], pool_scale.shape[1], w_c_out.shape[1]
    proj = matmul(h, w_in, l, BF16, resident_a=True)
    proj3 = proj.reshape(bsz, s, proj.shape[1])
    ya = short_conv(proj3, conv_w[l]).reshape(n, cw)
    yp = pool_mix(proj3, 3 * cw, pool_w[l], pool_scale[l]).reshape(n, pw)
    yc = diff_attention(proj3, 3 * cw + pw, lam[l], subln[l], lambda_init).reshape(n, aw)
    merged = merge(ya, yp, yc, w_a_out, w_p_out, w_c_out, l, proj, 3 * cw + pw + 3 * aw)
    return matmul(merged, w_o, l, F32, res=x, tn=256)


def kernel(x, norm_mix, w_in, conv_w, w_a_out, pool_w, pool_scale, w_p_out, lam, subln, w_c_out, w_o,
           norm_ffn, ffn_w_gate, ffn_w_up, ffn_w_down, w_router, moe_w_gate, moe_w_up, moe_w_down,
           norm_final):
    bsz, s, d = x.shape
    depth = norm_mix.shape[0]
    n = bsz * s
    x = x.reshape(n, d)
    for l in range(depth):
        lambda_init = 0.8 - 0.6 * math.exp(-0.3 * l)
        h = rmsnorm(x, norm_mix[l], BF16)
        x = _mixer(x, h, l, w_in, conv_w, w_a_out, pool_w, pool_scale, w_p_out, lam, subln, w_c_out, w_o,
                   lambda_init, bsz, s)
        j = l // 2
        if l % 2 == 0:
            h = rmsnorm(x, norm_ffn[l], BF16)
            act = glu_in(h, ffn_w_gate, ffn_w_up, j)
            x = matmul_acc(act, ffn_w_down, j, x)
        elif l < depth - 1:
            x = moe_ffn(x, norm_ffn[l], w_router[j], moe_w_gate, moe_w_up, moe_w_down, j)
        else:
            return moe_ffn(x, norm_ffn[l], w_router[j], moe_w_gate, moe_w_up, moe_w_down, j,
                           norm_g=norm_final).reshape(bsz, s, d)
    return rmsnorm(x, norm_final, F32).reshape(bsz, s, d)
```

```python
import functools
import math

import jax
import jax.numpy as jnp
from jax import lax
from jax.experimental import pallas as pl
from jax.experimental.pallas import tpu as pltpu

F32 = jnp.float32
BF16 = jnp.bfloat16
U32 = jnp.uint32

NORM_EPS = 1e-6
SUBLN_EPS = 1e-5
CONV_K = 3
POOL_WINDOWS = (2, 4, 8, 16)
DA_HEADS = 8
DA_HEAD_DIM = 128
N_BRANCHES = 3
TOP_K = 2

LANES = 128
BF16_ROWS = 16
MXU_EDGE = 256
VMEM_BYTES = 64 * 1024 * 1024
VMEM_LIMIT = VMEM_BYTES - 8 * 1024 * 1024

HALO = BF16_ROWS
assert HALO >= max(POOL_WINDOWS) and HALO >= CONV_K
LOG2E = math.log2(math.e)
HI16 = 0xFFFF0000


def _tile(dim, pref, unit=LANES):
    if dim <= pref:
        return dim
    t = (pref // unit) * unit
    while t > unit and dim % t:
        t -= unit
    assert dim % t == 0, (dim, pref, unit)
    return t


def _params(sem):
    return pltpu.CompilerParams(dimension_semantics=sem, vmem_limit_bytes=VMEM_LIMIT)


def _resident(block_shape, index_map):
    return pl.BlockSpec(block_shape, index_map, pipeline_mode=pl.Buffered(1))


def _rms(x, g):
    ms = jnp.mean(x * x, axis=-1, keepdims=True)
    return x * lax.rsqrt(ms + NORM_EPS) * g


def _rmsnorm_body(x_ref, g_ref, o_ref):
    o_ref[...] = _rms(x_ref[...], g_ref[...]).astype(o_ref.dtype)


def rmsnorm(x, g, out_dtype):
    n, d = x.shape
    tr = _tile(n, 256, 8)
    return pl.pallas_call(
        _rmsnorm_body,
        grid=(n // tr,),
        in_specs=[pl.BlockSpec((tr, d), lambda i: (i, 0)),
                  pl.BlockSpec((1, d), lambda i: (0, 0))],
        out_specs=pl.BlockSpec((tr, d), lambda i: (i, 0)),
        out_shape=jax.ShapeDtypeStruct((n, d), out_dtype),
        compiler_params=_params(("parallel",)),
        name="rmsnorm",
    )(x, g.reshape(1, d))


def _pack_halves(h):
    half = h.shape[1] // 2
    lo = lax.bitcast_convert_type(h[:, :half].astype(BF16).astype(F32), U32)
    hi = lax.bitcast_convert_type(h[:, half:].astype(BF16).astype(F32), U32)
    return (lo >> 16) | (hi & jnp.uint32(HI16))


def _unpack_halves(p, dtype):
    lo = lax.bitcast_convert_type(p << 16, F32).astype(dtype)
    hi = lax.bitcast_convert_type(p & jnp.uint32(HI16), F32).astype(dtype)
    return lo, hi


def _split_bf16(x):
    hi = x.astype(BF16)
    return hi, (x - hi.astype(F32)).astype(BF16)


def _rmsnorm_router_body(x_ref, g_ref, wr_ref, hp_ref, ei_ref, ew_ref, *, n_experts):
    h = _rms(x_ref[...], g_ref[...])
    hp_ref[...] = _pack_halves(h)
    h_hi, h_lo = _split_bf16(h)
    w_hi, w_lo = _split_bf16(wr_ref[...])
    logits = (jnp.dot(h_hi, w_hi, preferred_element_type=F32) + jnp.dot(h_hi, w_lo, preferred_element_type=F32)
              + jnp.dot(h_lo, w_hi, preferred_element_type=F32))
    lane = lax.broadcasted_iota(jnp.int32, logits.shape, 1)
    neg = jnp.float32(-jnp.inf)
    logits = jnp.where(lane < n_experts, logits, neg)
    m1 = jnp.max(logits, axis=-1, keepdims=True)
    i1 = jnp.min(jnp.where(logits == m1, lane, LANES), axis=-1, keepdims=True)
    rest = jnp.where(lane == i1, neg, logits)
    m2 = jnp.max(rest, axis=-1, keepdims=True)
    i2 = jnp.min(jnp.where(rest == m2, lane, LANES), axis=-1, keepdims=True)
    e2 = jnp.exp(m2 - m1)
    w1 = 1.0 / (1.0 + e2)
    w2 = e2 / (1.0 + e2)
    ei_ref[...] = jnp.where(lane == 0, i1, jnp.where(lane == 1, i2, 0))
    ew_ref[...] = jnp.where(lane == 0, w1, jnp.where(lane == 1, w2, 0.0))


def rmsnorm_router(x, g, w_router):
    n, d = x.shape
    n_experts = w_router.shape[1]
    wr = jnp.pad(w_router, ((0, 0), (0, LANES - n_experts)))
    tr = _tile(n, 256, 8)
    return pl.pallas_call(
        functools.partial(_rmsnorm_router_body, n_experts=n_experts),
        grid=(n // tr,),
        in_specs=[pl.BlockSpec((tr, d), lambda i: (i, 0)),
                  pl.BlockSpec((1, d), lambda i: (0, 0)),
                  pl.BlockSpec((d, LANES), lambda i: (0, 0))],
        out_specs=[pl.BlockSpec((tr, d // 2), lambda i: (i, 0)),
                   pl.BlockSpec((tr, LANES), lambda i: (i, 0)),
                   pl.BlockSpec((tr, LANES), lambda i: (i, 0))],
        out_shape=[jax.ShapeDtypeStruct((n, d // 2), U32),
                   jax.ShapeDtypeStruct((n, LANES), jnp.int32),
                   jax.ShapeDtypeStruct((n, LANES), F32)],
        compiler_params=_params(("parallel",)),
        name="rmsnorm_router",
    )(x, g.reshape(1, d), wr)


ACC_ROWS = 512


def _mm_body(a_ref, w_ref, *rest, has_res):
    o_ref = rest[-1]
    w = w_ref[...].astype(BF16)
    rc = min(ACC_ROWS, o_ref.shape[0])
    for c in range(o_ref.shape[0] // rc):
        rows = slice(c * rc, (c + 1) * rc)
        r = jnp.dot(a_ref[rows], w, preferred_element_type=F32)
        if has_res:
            r = r + rest[0][rows]
        o_ref[rows] = r.astype(o_ref.dtype)


def matmul(a, w, layer, out_dtype, res=None, tm=2048, tn=512, resident_a=False):
    m, kd = a.shape
    n = w.shape[2]
    tm, tn = _tile(m, tm), _tile(n, tn)
    a_spec = _resident if resident_a else pl.BlockSpec
    in_specs = [a_spec((tm, kd), lambda i, j: (i, 0)),
                pl.BlockSpec((None, kd, tn), lambda i, j: (layer, 0, j))]
    args = [a, w]
    if res is not None:
        in_specs.append(pl.BlockSpec((tm, tn), lambda i, j: (i, j)))
        args.append(res)
    return pl.pallas_call(
        functools.partial(_mm_body, has_res=res is not None),
        grid=(m // tm, n // tn),
        in_specs=in_specs,
        out_specs=pl.BlockSpec((tm, tn), lambda i, j: (i, j)),
        out_shape=jax.ShapeDtypeStruct((m, n), out_dtype),
        compiler_params=_params(("parallel", "arbitrary")),
        name="matmul_res" if res is not None else "matmul",
    )(*args)


def _mm_acc_body(a_ref, w_ref, res_ref, o_ref):
    @pl.when(pl.program_id(2) == 0)
    def _():
        o_ref[...] = res_ref[...]

    w = w_ref[...].astype(BF16)
    rc = min(ACC_ROWS, o_ref.shape[0])
    for r in range(o_ref.shape[0] // rc):
        rows = slice(r * rc, (r + 1) * rc)
        o_ref[rows] += jnp.dot(a_ref[rows], w, preferred_element_type=F32)


def matmul_acc(a, w, layer, res, tm=2048, tn=1024, tk=1024):
    m, kd = a.shape
    n = w.shape[2]
    tm, tn, tk = _tile(m, tm), _tile(n, tn, MXU_EDGE), _tile(kd, tk, MXU_EDGE)
    return pl.pallas_call(
        _mm_acc_body,
        grid=(m // tm, n // tn, kd // tk),
        in_specs=[pl.BlockSpec((tm, tk), lambda i, j, k: (i, k)),
                  pl.BlockSpec((None, tk, tn), lambda i, j, k: (layer, k, j)),
                  pl.BlockSpec((tm, tn), lambda i, j, k: (i, j))],
        out_specs=pl.BlockSpec((tm, tn), lambda i, j, k: (i, j)),
        out_shape=jax.ShapeDtypeStruct((m, n), F32),
        compiler_params=_params(("parallel", "parallel", "arbitrary")),
        name="matmul_acc",
    )(a, w, res)


def _silu_mul(g, u):
    return g * jax.nn.sigmoid(g) * u


def _glu_body(a_ref, wg_ref, wu_ref, o_ref):
    wg, wu = wg_ref[...].astype(BF16), wu_ref[...].astype(BF16)
    rc = min(ACC_ROWS, o_ref.shape[0])
    for c in range(o_ref.shape[0] // rc):
        rows = slice(c * rc, (c + 1) * rc)
        a = a_ref[rows]
        g = jnp.dot(a, wg, preferred_element_type=F32)
        u = jnp.dot(a, wu, preferred_element_type=F32)
        o_ref[rows] = _silu_mul(g, u).astype(o_ref.dtype)


def glu_in(a, wg, wu, layer, tm=2048, tn=256):
    m, kd = a.shape
    f = wg.shape[2]
    tm, tn = _tile(m, tm), _tile(f, tn, MXU_EDGE)
    w_spec = pl.BlockSpec((None, kd, tn), lambda i, j: (layer, 0, j))
    return pl.pallas_call(
        _glu_body,
        grid=(m // tm, f // tn),
        in_specs=[pl.BlockSpec((tm, kd), lambda i, j: (i, 0)), w_spec, w_spec],
        out_specs=pl.BlockSpec((tm, tn), lambda i, j: (i, j)),
        out_shape=jax.ShapeDtypeStruct((m, f), BF16),
        compiler_params=_params(("parallel", "arbitrary")),
        name="glu_in",
    )(a, wg, wu)


MERGE_ROWS = 256


def _merge_body(a_ref, p_ref, c_ref, wa_ref, wp_ref, wc_ref, g0_ref, g1_ref, g2_ref, o_ref):
    branches = ((a_ref, wa_ref[...].astype(BF16), g0_ref),
                (p_ref, wp_ref[...].astype(BF16), g1_ref),
                (c_ref, wc_ref[...].astype(BF16), g2_ref))
    rc = min(MERGE_ROWS, o_ref.shape[0])
    for r in range(o_ref.shape[0] // rc):
        rows = slice(r * rc, (r + 1) * rc)
        acc = None
        for x_ref, w, g_ref in branches:
            y = jax.nn.sigmoid(g_ref[rows].astype(F32)) * jnp.dot(x_ref[rows], w, preferred_element_type=F32)
            acc = y if acc is None else acc + y
        o_ref[rows] = acc.astype(o_ref.dtype)


def merge(ya, yp, yc, wa, wp, wc, layer, proj, gate_col, tm=2048, tn=256):
    m = ya.shape[0]
    d = wa.shape[2]
    tm, tn = _tile(m, tm), _tile(d, tn)
    assert gate_col % tn == 0
    gb, nd = gate_col // tn, d // tn

    def x_spec(x):
        return pl.BlockSpec((tm, x.shape[1]), lambda i, j: (i, 0))

    def w_spec(w):
        return pl.BlockSpec((None, w.shape[1], tn), lambda i, j: (layer, 0, j))

    def g_spec(b):
        return pl.BlockSpec((tm, tn), lambda i, j: (i, gb + b * nd + j))

    return pl.pallas_call(
        _merge_body,
        grid=(m // tm, nd),
        in_specs=[x_spec(ya), x_spec(yp), x_spec(yc), w_spec(wa), w_spec(wp), w_spec(wc),
                  g_spec(0), g_spec(1), g_spec(2)],
        out_specs=pl.BlockSpec((tm, tn), lambda i, j: (i, j)),
        out_shape=jax.ShapeDtypeStruct((m, d), BF16),
        compiler_params=_params(("parallel", "parallel")),
        name="merge",
    )(ya, yp, yc, wa, wp, wc, proj, proj, proj)


def _with_halo(prev, cur, first):
    prev = jnp.where(first, 0.0, prev)
    return jnp.concatenate([prev, cur], axis=0)


def _conv_body(u_ref, b_ref, c_ref, up_ref, cp_ref, w_ref, o_ref):
    first = pl.program_id(1) == 0
    z = c_ref[...].astype(F32) * u_ref[...].astype(F32)
    zp = cp_ref[...].astype(F32) * up_ref[...].astype(F32)
    ze = _with_halo(zp, z, first)
    z1 = pltpu.roll(ze, 1, 0)[HALO:]
    z2 = pltpu.roll(ze, 2, 0)[HALO:]
    w = w_ref[...]
    zc = w[0:1] * z2 + w[1:2] * z1 + w[2:3] * z
    o_ref[...] = (b_ref[...].astype(F32) * zc).astype(o_ref.dtype)


def short_conv(proj, conv_w, ts=2048, cw=256):
    bsz, s, _ = proj.shape
    width = conv_w.shape[1]
    ts, cw = _tile(s, ts, HALO), _tile(width, cw)
    nc, hb = width // cw, ts // HALO

    def cur(off):
        return pl.BlockSpec((None, ts, cw), lambda b, i, j: (b, i, off * nc + j))

    def prev(off):
        return pl.BlockSpec((None, HALO, cw), lambda b, i, j: (b, jnp.maximum(i * hb - 1, 0), off * nc + j))

    return pl.pallas_call(
        _conv_body,
        grid=(bsz, s // ts, nc),
        in_specs=[cur(0), cur(1), cur(2), prev(0), prev(2),
                  pl.BlockSpec((CONV_K, cw), lambda b, i, j: (0, j))],
        out_specs=pl.BlockSpec((None, ts, cw), lambda b, i, j: (b, i, j)),
        out_shape=jax.ShapeDtypeStruct((bsz, s, width), BF16),
        compiler_params=_params(("parallel", "parallel", "parallel")),
        name="short_conv",
    )(proj, proj, proj, proj, proj, conv_w)


def _pool_body(u_ref, up_ref, pw_ref, ps_ref, o_ref, *, ts):
    i, g = pl.program_id(1), pl.program_id(2)
    u = u_ref[...].astype(F32)
    ue = _with_halo(up_ref[...].astype(F32), u, i == 0)
    t1 = i * ts + lax.broadcasted_iota(jnp.int32, u.shape, 0) + 1
    for gi, w in enumerate(POOL_WINDOWS):
        assert w & (w - 1) == 0 and w <= HALO

        @pl.when(g == gi)
        def _(w=w):
            s, span = ue, 1
            while span < w:
                s = s + pltpu.roll(s, span, 0)
                span *= 2
            mean = s[HALO:] / jnp.minimum(t1, w).astype(F32)
            mixed = (mean - u).astype(BF16)
            y = jnp.dot(mixed, pw_ref[...].astype(BF16), preferred_element_type=F32)
            o_ref[...] = (y * ps_ref[...]).astype(o_ref.dtype)


def pool_mix(proj, col, pool_w, pool_scale, ts=2048):
    bsz, s, _ = proj.shape
    n_g, gd, _ = pool_w.shape
    assert n_g == len(POOL_WINDOWS) and col % gd == 0
    ts = _tile(s, ts, HALO)
    cb, hb = col // gd, ts // HALO
    return pl.pallas_call(
        functools.partial(_pool_body, ts=ts),
        grid=(bsz, s // ts, n_g),
        in_specs=[pl.BlockSpec((None, ts, gd), lambda b, i, g: (b, i, cb + g)),
                  pl.BlockSpec((None, HALO, gd), lambda b, i, g: (b, jnp.maximum(i * hb - 1, 0), cb + g)),
                  pl.BlockSpec((None, gd, gd), lambda b, i, g: (g, 0, 0)),
                  pl.BlockSpec((1, gd), lambda b, i, g: (0, g))],
        out_specs=pl.BlockSpec((None, ts, gd), lambda b, i, g: (b, i, g)),
        out_shape=jax.ShapeDtypeStruct((bsz, s, n_g * gd), BF16),
        compiler_params=_params(("parallel", "parallel", "parallel")),
        name="pool_mix",
    )(proj, proj, pool_w, pool_scale.reshape(1, n_g * gd))


CHUNK_GROUP = 2


def _for_chunks(n, fn):
    def group(jj, carry):
        for u in range(CHUNK_GROUP):
            fn(CHUNK_GROUP * jj + u)
        return carry

    def single(j, carry):
        fn(j)
        return carry

    lax.fori_loop(0, n // CHUNK_GROUP, group, 0)
    lax.fori_loop(n - n % CHUNK_GROUP, n, single, 0)


def _attn_body(lam_ref, g_ref, q_ref, k_ref, v_ref, o_ref, s_ref, m_ref, l_ref, acc_ref, *, tq, lambda_init):
    dh = DA_HEAD_DIM
    i = pl.program_id(2)
    maps = range(2)
    slabs = [slice(b * LANES, (b + 1) * LANES) for b in range(tq // LANES)]
    q = (q_ref[...].astype(F32) * (dh ** -0.5 * LOG2E)).astype(BF16)

    def rows(j):
        return pl.ds(pl.multiple_of(j * tq, tq), tq)

    def scores(j, c):
        cols = slice(c * dh, (c + 1) * dh)
        return lax.dot_general(q[:, cols], k_ref[rows(j), cols], (((1,), (1,)), ((), ())),
                               preferred_element_type=F32)

    def keep(j, c, s):
        s_ref[c, j] = s
        m = m_ref[c]
        for sl in slabs:
            m = jnp.maximum(m, s[:, sl])
        m_ref[c] = m

    def pass1(j):
        for c in maps:
            keep(j, c, scores(j, c))

    m_ref[...] = jnp.full_like(m_ref, -jnp.inf)
    _for_chunks(i, pass1)
    row = lax.broadcasted_iota(jnp.int32, (tq, tq), 0)
    col = lax.broadcasted_iota(jnp.int32, (tq, tq), 1)
    for c in maps:
        keep(i, c, jnp.where(col <= row, scores(i, c), -jnp.inf))
    mb = [jnp.broadcast_to(jnp.max(m_ref[c], axis=-1, keepdims=True), (tq, LANES)) for c in maps]

    def pass2(j):
        p_maps = []
        for c in maps:
            s = s_ref[c, j]
            l = l_ref[c]
            ps = []
            for sl in slabs:
                p = jnp.exp2(s[:, sl] - mb[c])
                l = l + p
                ps.append(p.astype(BF16))
            l_ref[c] = l
            p_maps.append(jnp.concatenate(ps, axis=1))
        acc_ref[...] += jnp.dot(jnp.concatenate(p_maps, axis=0), v_ref[rows(j), :], preferred_element_type=F32)

    l_ref[...] = jnp.zeros_like(l_ref)
    acc_ref[...] = jnp.zeros_like(acc_ref)
    _for_chunks(i + 1, pass2)
    outs = [acc_ref[c * tq:(c + 1) * tq] / jnp.sum(l_ref[c], axis=-1, keepdims=True) for c in maps]

    lp = lam_ref[...]
    lam = (jnp.exp(jnp.sum(lp[0:1] * lp[1:2], axis=-1, keepdims=True))
           - jnp.exp(jnp.sum(lp[2:3] * lp[3:4], axis=-1, keepdims=True)) + lambda_init)
    o = outs[0] - lam * outs[1]
    o = o * lax.rsqrt(jnp.mean(o * o, axis=-1, keepdims=True) + SUBLN_EPS)
    o_ref[...] = (o * g_ref[...] * (1.0 - lambda_init)).astype(o_ref.dtype)


def diff_attention(proj, col, lam, subln_g, lambda_init, tq=512):
    bsz, s, _ = proj.shape
    hw = 2 * DA_HEAD_DIM
    assert col % hw == 0
    tq = _tile(s, tq)
    qb = col // hw
    kb, vb = qb + DA_HEADS, qb + 2 * DA_HEADS
    return pl.pallas_call(
        functools.partial(_attn_body, tq=tq, lambda_init=lambda_init),
        grid=(bsz, DA_HEADS, s // tq),
        in_specs=[pl.BlockSpec((4, DA_HEAD_DIM), lambda b, h, i: (0, 0)),
                  pl.BlockSpec((1, hw), lambda b, h, i: (0, 0)),
                  pl.BlockSpec((None, tq, hw), lambda b, h, i: (b, i, qb + h)),
                  pl.BlockSpec((None, s, hw), lambda b, h, i: (b, 0, kb + h)),
                  pl.BlockSpec((None, s, hw), lambda b, h, i: (b, 0, vb + h))],
        out_specs=pl.BlockSpec((None, tq, hw), lambda b, h, i: (b, i, h)),
        out_shape=jax.ShapeDtypeStruct((bsz, s, DA_HEADS * hw), BF16),
        scratch_shapes=[pltpu.VMEM((2, s // tq, tq, tq), F32), pltpu.VMEM((2, tq, LANES), F32),
                        pltpu.VMEM((2, tq, LANES), F32), pltpu.VMEM((2 * tq, hw), F32)],
        compiler_params=_params(("parallel", "parallel", "arbitrary")),
        name="diff_attention",
    )(lam, subln_g.reshape(1, hw), proj, proj, proj)


MOE_ROWS = 512
MOE_TOKENS = 256


def _route(ei, n_e):
    n = ei.shape[0]
    e = ei[:, :TOP_K]
    onehot = (e[:, :, None] == jnp.arange(n_e, dtype=jnp.int32)).astype(jnp.int32).reshape(n * TOP_K, n_e)
    incl = jnp.cumsum(onehot, axis=0)
    rank = jnp.sum((incl - onehot) * onehot, axis=1)
    tiles = (incl[-1] + MOE_ROWS - 1) // MOE_ROWS
    tile_end = jnp.cumsum(tiles)
    start = (tile_end - tiles) * MOE_ROWS
    dest = (jnp.sum(onehot * start[None, :], axis=1) + rank).reshape(n, TOP_K)
    max_tiles = (n * TOP_K) // MOE_ROWS + n_e
    n_used = tile_end[-1:]
    t = jnp.minimum(jnp.arange(max_tiles, dtype=jnp.int32), n_used[0] - 1)
    tile_expert = jnp.sum((t[:, None] >= tile_end[None, :]).astype(jnp.int32), axis=1)
    return dest.astype(jnp.int32), tile_expert.astype(jnp.int32), n_used.astype(jnp.int32), max_tiles


def _dispatch_body(dest_ref, hp_ref, init_ref, hs_ref, sem, *, tt):
    del init_ref

    def row_copy(t, k):
        return pltpu.make_async_copy(hp_ref.at[pl.ds(t, 1)], hs_ref.at[pl.ds(dest_ref[0, TOP_K * t + k], 1)], sem)

    def start(t, carry):
        for k in range(TOP_K):
            row_copy(t, k).start()
        return carry

    lax.fori_loop(0, tt, start, 0, unroll=4)
    all_rows = hs_ref.at[pl.ds(0, TOP_K * tt)]
    pltpu.make_async_copy(all_rows, all_rows, sem).wait()


def moe_dispatch(hp, dest, n_rows):
    n, w = hp.shape
    tt = _tile(n, MOE_TOKENS, 8)
    return pl.pallas_call(
        functools.partial(_dispatch_body, tt=tt),
        grid=(n // tt,),
        in_specs=[pl.BlockSpec((None, 1, TOP_K * tt), lambda i: (i, 0, 0), memory_space=pltpu.SMEM),
                  pl.BlockSpec((tt, w), lambda i: (i, 0)),
                  pl.BlockSpec(memory_space=pl.ANY)],
        out_specs=pl.BlockSpec(memory_space=pl.ANY),
        out_shape=jax.ShapeDtypeStruct((n_rows, w), U32),
        scratch_shapes=[pltpu.SemaphoreType.DMA(())],
        input_output_aliases={2: 0},
        compiler_params=_params(("arbitrary",)),
        name="moe_dispatch",
    )(dest.reshape(n // tt, 1, TOP_K * tt), hp, jnp.zeros((n_rows, w), U32))


def _zero_unused_tile(nu_ref, o_ref):
    @pl.when(pl.program_id(1) >= nu_ref[0])
    def _():
        o_ref[...] = jnp.zeros_like(o_ref)


def _moe_glu_body(te_ref, nu_ref, a_ref, wg_ref, wu_ref, o_ref):
    del te_ref
    _zero_unused_tile(nu_ref, o_ref)

    @pl.when(pl.program_id(1) < nu_ref[0])
    def _():
        lo, hi = _unpack_halves(a_ref[...], BF16)
        half = lo.shape[1]

        def proj(w_ref):
            return (jnp.dot(lo, w_ref[:half].astype(BF16), preferred_element_type=F32)
                    + jnp.dot(hi, w_ref[half:].astype(BF16), preferred_element_type=F32))

        o_ref[...] = _silu_mul(proj(wg_ref), proj(wu_ref)).astype(o_ref.dtype)


def moe_glu(hs, wg, wu, layer, tile_expert, n_used, tn=512):
    rows, half = hs.shape
    f = wg.shape[3]
    tn = _tile(f, tn, MXU_EDGE)
    n_tiles = rows // MOE_ROWS

    def row_tile(r, nu):
        return jnp.minimum(r, nu[0] - 1)

    w_spec = pl.BlockSpec((None, None, 2 * half, tn), lambda j, r, te, nu: (layer, te[r], 0, j))
    return pl.pallas_call(
        _moe_glu_body,
        grid_spec=pltpu.PrefetchScalarGridSpec(
            num_scalar_prefetch=2,
            grid=(f // tn, n_tiles),
            in_specs=[pl.BlockSpec((MOE_ROWS, half), lambda j, r, te, nu: (row_tile(r, nu), 0)), w_spec, w_spec],
            out_specs=pl.BlockSpec((MOE_ROWS, tn), lambda j, r, te, nu: (r, j))),
        out_shape=jax.ShapeDtypeStruct((rows, f), BF16),
        compiler_params=_params(("arbitrary", "arbitrary")),
        name="moe_glu",
    )(tile_expert, n_used, hs, wg, wu)


def _moe_down_body(te_ref, nu_ref, a_ref, wlo_ref, whi_ref, o_ref):
    del te_ref
    _zero_unused_tile(nu_ref, o_ref)

    @pl.when(pl.program_id(1) < nu_ref[0])
    def _():
        a = a_ref[...]
        ys = [jnp.dot(a, w_ref[...].astype(BF16), preferred_element_type=F32) for w_ref in (wlo_ref, whi_ref)]
        o_ref[...] = _pack_halves(jnp.concatenate(ys, axis=1))


def moe_down(act, wd, layer, tile_expert, n_used, tn=512):
    rows, f = act.shape
    half = wd.shape[3] // 2
    tn = _tile(half, tn)
    n_tiles, nh = rows // MOE_ROWS, half // tn

    def row_tile(r, nu):
        return jnp.minimum(r, nu[0] - 1)

    def w_spec(hi):
        return pl.BlockSpec((None, None, f, tn), lambda j, r, te, nu: (layer, te[r], 0, j + hi * nh))

    return pl.pallas_call(
        _moe_down_body,
        grid_spec=pltpu.PrefetchScalarGridSpec(
            num_scalar_prefetch=2,
            grid=(nh, n_tiles),
            in_specs=[pl.BlockSpec((MOE_ROWS, f), lambda j, r, te, nu: (row_tile(r, nu), 0)), w_spec(0), w_spec(1)],
            out_specs=pl.BlockSpec((MOE_ROWS, tn), lambda j, r, te, nu: (r, j))),
        out_shape=jax.ShapeDtypeStruct((rows, half), U32),
        compiler_params=_params(("arbitrary", "arbitrary")),
        name="moe_down",
    )(tile_expert, n_used, act, wd, wd)


def _combine_body(dest_ref, next_ref, ew_ref, x_ref, *rest, tt, has_norm):
    if has_norm:
        g_ref, ys_ref, o_ref, ybuf, sem = rest
    else:
        ys_ref, o_ref, ybuf, sem = rest
    i, n_steps = pl.program_id(0), pl.num_programs(0)
    slot = i % 2

    def row_copy(d_ref, sl, t, k):
        return pltpu.make_async_copy(ys_ref.at[pl.ds(d_ref[0, TOP_K * t + k], 1)],
                                     ybuf.at[sl, k, pl.ds(t, 1)], sem.at[sl])

    def start_all(d_ref, sl):
        def start(t, carry):
            for k in range(TOP_K):
                row_copy(d_ref, sl, t, k).start()
            return carry

        lax.fori_loop(0, tt, start, 0, unroll=4)

    @pl.when(i == 0)
    def _():
        start_all(dest_ref, slot)

    @pl.when(i + 1 < n_steps)
    def _():
        start_all(next_ref, 1 - slot)

    pltpu.make_async_copy(ybuf.at[slot], ybuf.at[slot], sem.at[slot]).wait()
    ew = ew_ref[...]
    y = x_ref[...]
    for k in range(TOP_K):
        y = y + ew[:, k:k + 1] * jnp.concatenate(_unpack_halves(ybuf[slot, k], F32), axis=1)
    o_ref[...] = _rms(y, g_ref[...]) if has_norm else y


def moe_combine(x, ys, dest, ew, norm_g=None):
    n, d = x.shape
    tt = _tile(n, MOE_TOKENS, 8)
    n_steps = n // tt
    dest = dest.reshape(n_steps, 1, TOP_K * tt)

    def d_spec(ahead):
        return pl.BlockSpec((None, 1, TOP_K * tt), lambda i: (jnp.minimum(i + ahead, n_steps - 1), 0, 0),
                            memory_space=pltpu.SMEM)

    in_specs = [d_spec(0), d_spec(1),
                pl.BlockSpec((tt, LANES), lambda i: (i, 0)),
                pl.BlockSpec((tt, d), lambda i: (i, 0))]
    args = [dest, dest, ew, x]
    if norm_g is not None:
        in_specs.append(pl.BlockSpec((1, d), lambda i: (0, 0)))
        args.append(norm_g.reshape(1, d))
    return pl.pallas_call(
        functools.partial(_combine_body, tt=tt, has_norm=norm_g is not None),
        grid=(n_steps,),
        in_specs=in_specs + [pl.BlockSpec(memory_space=pl.ANY)],
        out_specs=pl.BlockSpec((tt, d), lambda i: (i, 0)),
        out_shape=jax.ShapeDtypeStruct((n, d), F32),
        scratch_shapes=[pltpu.VMEM((2, TOP_K, tt, d // 2), U32), pltpu.SemaphoreType.DMA((2,))],
        compiler_params=_params(("arbitrary",)),
        name="moe_combine",
    )(*args, ys)


def moe_ffn(x, g, w_router, wg, wu, wd, layer, norm_g=None):
    n_e = wg.shape[1]
    hp, ei, ew = rmsnorm_router(x, g, w_router)
    dest, tile_expert, n_used, max_tiles = _route(ei, n_e)
    hs = moe_dispatch(hp, dest, max_tiles * MOE_ROWS)
    act = moe_glu(hs, wg, wu, layer, tile_expert, n_used)
    ys = moe_down(act, wd, layer, tile_expert, n_used)
    return moe_combine(x, ys, dest, ew, norm_g)


def _mixer(x, h, l, w_in, conv_w, w_a_out, pool_w, pool_scale, w_p_out, lam, subln, w_c_out, w_o,
           lambda_init, bsz, s):
    n, d = x.shape
    cw, pw, aw = conv_w.shape[2], pool_scale.shape[1], w_c_out.shape[1]
    proj = matmul(h, w_in, l, BF16, resident_a=True)
    proj3 = proj.reshape(bsz, s, proj.shape[1])
    ya = short_conv(proj3, conv_w[l]).reshape(n, cw)
    yp = pool_mix(proj3, 3 * cw, pool_w[l], pool_scale[l]).reshape(n, pw)
    yc = diff_attention(proj3, 3 * cw + pw, lam[l], subln[l], lambda_init).reshape(n, aw)
    merged = merge(ya, yp, yc, w_a_out, w_p_out, w_c_out, l, proj, 3 * cw + pw + 3 * aw)
    return matmul(merged, w_o, l, F32, res=x, tn=256)


def kernel(x, norm_mix, w_in, conv_w, w_a_out, pool_w, pool_scale, w_p_out, lam, subln, w_c_out, w_o,
           norm_ffn, ffn_w_gate, ffn_w_up, ffn_w_down, w_router, moe_w_gate, moe_w_up, moe_w_down,
           norm_final):
    bsz, s, d = x.shape
    depth = norm_mix.shape[0]
    n = bsz * s
    x = x.reshape(n, d)
    for l in range(depth):
        lambda_init = 0.8 - 0.6 * math.exp(-0.3 * l)
        h = rmsnorm(x, norm_mix[l], BF16)
        x = _mixer(x, h, l, w_in, conv_w, w_a_out, pool_w, pool_scale, w_p_out, lam, subln, w_c_out, w_o,
                   lambda_init, bsz, s)
        j = l // 2
        if l % 2 == 0:
            h = rmsnorm(x, norm_ffn[l], BF16)
            act = glu_in(h, ffn_w_gate, ffn_w_up, j)
            x = matmul_acc(act, ffn_w_down, j, x)
        elif l < depth - 1:
            x = moe_ffn(x, norm_ffn[l], w_router[j], moe_w_gate, moe_w_up, moe_w_down, j)
        else:
            return moe_ffn(x, norm_ffn[l], w_router[j], moe_w_gate, moe_w_up, moe_w_down, j,
                           norm_g=norm_final).reshape(bsz, s, d)
    return rmsnorm(x, norm_final, F32).reshape(bsz, s, d)
```

```python
import functools
import math

import jax
import jax.numpy as jnp
from jax import lax
from jax.experimental import pallas as pl
from jax.experimental.pallas import tpu as pltpu

F32 = jnp.float32
BF16 = jnp.bfloat16
U32 = jnp.uint32

NORM_EPS = 1e-6
SUBLN_EPS = 1e-5
CONV_K = 3
POOL_WINDOWS = (2, 4, 8, 16)
DA_HEADS = 8
DA_HEAD_DIM = 128
N_BRANCHES = 3
TOP_K = 2

LANES = 128
BF16_ROWS = 16
MXU_EDGE = 256
VMEM_BYTES = 64 * 1024 * 1024
VMEM_LIMIT = VMEM_BYTES - 8 * 1024 * 1024

HALO = BF16_ROWS
assert HALO >= max(POOL_WINDOWS) and HALO >= CONV_K
LOG2E = math.log2(math.e)
HI16 = 0xFFFF0000


def _tile(dim, pref, unit=LANES):
    if dim <= pref:
        return dim
    t = (pref // unit) * unit
    while t > unit and dim % t:
        t -= unit
    assert dim % t == 0, (dim, pref, unit)
    return t


def _params(sem):
    return pltpu.CompilerParams(dimension_semantics=sem, vmem_limit_bytes=VMEM_LIMIT)


def _rms(x, g):
    ms = jnp.mean(x * x, axis=-1, keepdims=True)
    return x * lax.rsqrt(ms + NORM_EPS) * g


def _rmsnorm_body(x_ref, g_ref, o_ref):
    o_ref[...] = _rms(x_ref[...], g_ref[...]).astype(o_ref.dtype)


def rmsnorm(x, g, out_dtype):
    n, d = x.shape
    tr = _tile(n, 256, 8)
    return pl.pallas_call(
        _rmsnorm_body,
        grid=(n // tr,),
        in_specs=[pl.BlockSpec((tr, d), lambda i: (i, 0)),
                  pl.BlockSpec((1, d), lambda i: (0, 0))],
        out_specs=pl.BlockSpec((tr, d), lambda i: (i, 0)),
        out_shape=jax.ShapeDtypeStruct((n, d), out_dtype),
        compiler_params=_params(("parallel",)),
        name="rmsnorm",
    )(x, g.reshape(1, d))


def _pack_halves(h):
    half = h.shape[1] // 2
    lo = lax.bitcast_convert_type(h[:, :half].astype(BF16).astype(F32), U32)
    hi = lax.bitcast_convert_type(h[:, half:].astype(BF16).astype(F32), U32)
    return (lo >> 16) | (hi & jnp.uint32(HI16))


def _unpack_halves(p):
    lo = lax.bitcast_convert_type(p << 16, F32).astype(BF16)
    hi = lax.bitcast_convert_type(p & jnp.uint32(HI16), F32).astype(BF16)
    return lo, hi


def _split_bf16(x):
    hi = x.astype(BF16)
    return hi, (x - hi.astype(F32)).astype(BF16)


def _rmsnorm_router_body(x_ref, g_ref, wr_ref, hp_ref, ei_ref, ew_ref, *, n_experts):
    h = _rms(x_ref[...], g_ref[...])
    hp_ref[...] = _pack_halves(h)
    h_hi, h_lo = _split_bf16(h)
    w_hi, w_lo = _split_bf16(wr_ref[...])
    logits = (jnp.dot(h_hi, w_hi, preferred_element_type=F32) + jnp.dot(h_hi, w_lo, preferred_element_type=F32)
              + jnp.dot(h_lo, w_hi, preferred_element_type=F32))
    lane = lax.broadcasted_iota(jnp.int32, logits.shape, 1)
    neg = jnp.float32(-jnp.inf)
    logits = jnp.where(lane < n_experts, logits, neg)
    m1 = jnp.max(logits, axis=-1, keepdims=True)
    i1 = jnp.min(jnp.where(logits == m1, lane, LANES), axis=-1, keepdims=True)
    rest = jnp.where(lane == i1, neg, logits)
    m2 = jnp.max(rest, axis=-1, keepdims=True)
    i2 = jnp.min(jnp.where(rest == m2, lane, LANES), axis=-1, keepdims=True)
    e2 = jnp.exp(m2 - m1)
    w1 = 1.0 / (1.0 + e2)
    w2 = e2 / (1.0 + e2)
    ei_ref[...] = jnp.where(lane == 0, i1, jnp.where(lane == 1, i2, 0))
    ew_ref[...] = jnp.where(lane == 0, w1, jnp.where(lane == 1, w2, 0.0))


def rmsnorm_router(x, g, w_router):
    n, d = x.shape
    n_experts = w_router.shape[1]
    wr = jnp.pad(w_router, ((0, 0), (0, LANES - n_experts)))
    tr = _tile(n, 256, 8)
    return pl.pallas_call(
        functools.partial(_rmsnorm_router_body, n_experts=n_experts),
        grid=(n // tr,),
        in_specs=[pl.BlockSpec((tr, d), lambda i: (i, 0)),
                  pl.BlockSpec((1, d), lambda i: (0, 0)),
                  pl.BlockSpec((d, LANES), lambda i: (0, 0))],
        out_specs=[pl.BlockSpec((tr, d // 2), lambda i: (i, 0)),
                   pl.BlockSpec((tr, LANES), lambda i: (i, 0)),
                   pl.BlockSpec((tr, LANES), lambda i: (i, 0))],
        out_shape=[jax.ShapeDtypeStruct((n, d // 2), U32),
                   jax.ShapeDtypeStruct((n, LANES), jnp.int32),
                   jax.ShapeDtypeStruct((n, LANES), F32)],
        compiler_params=_params(("parallel",)),
        name="rmsnorm_router",
    )(x, g.reshape(1, d), wr)


ACC_ROWS = 512


def _mm_body(a_ref, w_ref, *rest, has_res):
    o_ref = rest[-1]
    w = w_ref[...].astype(BF16)
    rc = min(ACC_ROWS, o_ref.shape[0])
    for c in range(o_ref.shape[0] // rc):
        rows = slice(c * rc, (c + 1) * rc)
        r = jnp.dot(a_ref[rows], w, preferred_element_type=F32)
        if has_res:
            r = r + rest[0][rows]
        o_ref[rows] = r.astype(o_ref.dtype)


def matmul(a, w, layer, out_dtype, res=None, tm=2048, tn=512):
    m, kd = a.shape
    n = w.shape[2]
    tm, tn = _tile(m, tm), _tile(n, tn)
    in_specs = [pl.BlockSpec((tm, kd), lambda i, j: (i, 0)),
                pl.BlockSpec((None, kd, tn), lambda i, j: (layer, 0, j))]
    args = [a, w]
    if res is not None:
        in_specs.append(pl.BlockSpec((tm, tn), lambda i, j: (i, j)))
        args.append(res)
    return pl.pallas_call(
        functools.partial(_mm_body, has_res=res is not None),
        grid=(m // tm, n // tn),
        in_specs=in_specs,
        out_specs=pl.BlockSpec((tm, tn), lambda i, j: (i, j)),
        out_shape=jax.ShapeDtypeStruct((m, n), out_dtype),
        compiler_params=_params(("parallel", "arbitrary")),
        name="matmul_res" if res is not None else "matmul",
    )(*args)


def _mm_acc_body(a_ref, w_ref, res_ref, o_ref):
    @pl.when(pl.program_id(2) == 0)
    def _():
        o_ref[...] = res_ref[...]

    w = w_ref[...].astype(BF16)
    rc = min(ACC_ROWS, o_ref.shape[0])
    for r in range(o_ref.shape[0] // rc):
        rows = slice(r * rc, (r + 1) * rc)
        o_ref[rows] += jnp.dot(a_ref[rows], w, preferred_element_type=F32)


def matmul_acc(a, w, layer, res, tm=2048, tn=1024, tk=1024):
    m, kd = a.shape
    n = w.shape[2]
    tm, tn, tk = _tile(m, tm), _tile(n, tn, MXU_EDGE), _tile(kd, tk, MXU_EDGE)
    return pl.pallas_call(
        _mm_acc_body,
        grid=(m // tm, n // tn, kd // tk),
        in_specs=[pl.BlockSpec((tm, tk), lambda i, j, k: (i, k)),
                  pl.BlockSpec((None, tk, tn), lambda i, j, k: (layer, k, j)),
                  pl.BlockSpec((tm, tn), lambda i, j, k: (i, j))],
        out_specs=pl.BlockSpec((tm, tn), lambda i, j, k: (i, j)),
        out_shape=jax.ShapeDtypeStruct((m, n), F32),
        compiler_params=_params(("parallel", "parallel", "arbitrary")),
        name="matmul_acc",
    )(a, w, res)


def _silu_mul(g, u):
    return g * jax.nn.sigmoid(g) * u


def _glu_body(a_ref, wg_ref, wu_ref, o_ref):
    wg, wu = wg_ref[...].astype(BF16), wu_ref[...].astype(BF16)
    rc = min(ACC_ROWS, o_ref.shape[0])
    for c in range(o_ref.shape[0] // rc):
        rows = slice(c * rc, (c + 1) * rc)
        a = a_ref[rows]
        g = jnp.dot(a, wg, preferred_element_type=F32)
        u = jnp.dot(a, wu, preferred_element_type=F32)
        o_ref[rows] = _silu_mul(g, u).astype(o_ref.dtype)


def glu_in(a, wg, wu, layer, tm=2048, tn=256):
    m, kd = a.shape
    f = wg.shape[2]
    tm, tn = _tile(m, tm), _tile(f, tn, MXU_EDGE)
    w_spec = pl.BlockSpec((None, kd, tn), lambda i, j: (layer, 0, j))
    return pl.pallas_call(
        _glu_body,
        grid=(m // tm, f // tn),
        in_specs=[pl.BlockSpec((tm, kd), lambda i, j: (i, 0)), w_spec, w_spec],
        out_specs=pl.BlockSpec((tm, tn), lambda i, j: (i, j)),
        out_shape=jax.ShapeDtypeStruct((m, f), BF16),
        compiler_params=_params(("parallel", "arbitrary")),
        name="glu_in",
    )(a, wg, wu)


MERGE_ROWS = 256


def _merge_body(a_ref, p_ref, c_ref, wa_ref, wp_ref, wc_ref, g0_ref, g1_ref, g2_ref, o_ref):
    branches = ((a_ref, wa_ref[...].astype(BF16), g0_ref),
                (p_ref, wp_ref[...].astype(BF16), g1_ref),
                (c_ref, wc_ref[...].astype(BF16), g2_ref))
    rc = min(MERGE_ROWS, o_ref.shape[0])
    for r in range(o_ref.shape[0] // rc):
        rows = slice(r * rc, (r + 1) * rc)
        acc = None
        for x_ref, w, g_ref in branches:
            y = jax.nn.sigmoid(g_ref[rows].astype(F32)) * jnp.dot(x_ref[rows], w, preferred_element_type=F32)
            acc = y if acc is None else acc + y
        o_ref[rows] = acc.astype(o_ref.dtype)


def merge(ya, yp, yc, wa, wp, wc, layer, proj, gate_col, tm=2048, tn=256):
    m = ya.shape[0]
    d = wa.shape[2]
    tm, tn = _tile(m, tm), _tile(d, tn)
    assert gate_col % tn == 0
    gb, nd = gate_col // tn, d // tn

    def x_spec(x):
        return pl.BlockSpec((tm, x.shape[1]), lambda i, j: (i, 0))

    def w_spec(w):
        return pl.BlockSpec((None, w.shape[1], tn), lambda i, j: (layer, 0, j))

    def g_spec(b):
        return pl.BlockSpec((tm, tn), lambda i, j: (i, gb + b * nd + j))

    return pl.pallas_call(
        _merge_body,
        grid=(m // tm, nd),
        in_specs=[x_spec(ya), x_spec(yp), x_spec(yc), w_spec(wa), w_spec(wp), w_spec(wc),
                  g_spec(0), g_spec(1), g_spec(2)],
        out_specs=pl.BlockSpec((tm, tn), lambda i, j: (i, j)),
        out_shape=jax.ShapeDtypeStruct((m, d), BF16),
        compiler_params=_params(("parallel", "parallel")),
        name="merge",
    )(ya, yp, yc, wa, wp, wc, proj, proj, proj)


def _with_halo(prev, cur, first):
    prev = jnp.where(first, 0.0, prev)
    return jnp.concatenate([prev, cur], axis=0)


def _conv_body(u_ref, b_ref, c_ref, up_ref, cp_ref, w_ref, o_ref):
    first = pl.program_id(1) == 0
    z = c_ref[...].astype(F32) * u_ref[...].astype(F32)
    zp = cp_ref[...].astype(F32) * up_ref[...].astype(F32)
    ze = _with_halo(zp, z, first)
    z1 = pltpu.roll(ze, 1, 0)[HALO:]
    z2 = pltpu.roll(ze, 2, 0)[HALO:]
    w = w_ref[...]
    zc = w[0:1] * z2 + w[1:2] * z1 + w[2:3] * z
    o_ref[...] = (b_ref[...].astype(F32) * zc).astype(o_ref.dtype)


def short_conv(proj, conv_w, ts=2048, cw=256):
    bsz, s, _ = proj.shape
    width = conv_w.shape[1]
    ts, cw = _tile(s, ts, HALO), _tile(width, cw)
    nc, hb = width // cw, ts // HALO

    def cur(off):
        return pl.BlockSpec((None, ts, cw), lambda b, i, j: (b, i, off * nc + j))

    def prev(off):
        return pl.BlockSpec((None, HALO, cw), lambda b, i, j: (b, jnp.maximum(i * hb - 1, 0), off * nc + j))

    return pl.pallas_call(
        _conv_body,
        grid=(bsz, s // ts, nc),
        in_specs=[cur(0), cur(1), cur(2), prev(0), prev(2),
                  pl.BlockSpec((CONV_K, cw), lambda b, i, j: (0, j))],
        out_specs=pl.BlockSpec((None, ts, cw), lambda b, i, j: (b, i, j)),
        out_shape=jax.ShapeDtypeStruct((bsz, s, width), BF16),
        compiler_params=_params(("parallel", "parallel", "parallel")),
        name="short_conv",
    )(proj, proj, proj, proj, proj, conv_w)


def _pool_body(u_ref, up_ref, pw_ref, ps_ref, o_ref, *, ts):
    i, g = pl.program_id(1), pl.program_id(2)
    u = u_ref[...].astype(F32)
    ue = _with_halo(up_ref[...].astype(F32), u, i == 0)
    t1 = i * ts + lax.broadcasted_iota(jnp.int32, u.shape, 0) + 1
    for gi, w in enumerate(POOL_WINDOWS):
        assert w & (w - 1) == 0 and w <= HALO

        @pl.when(g == gi)
        def _(w=w):
            s, span = ue, 1
            while span < w:
                s = s + pltpu.roll(s, span, 0)
                span *= 2
            mean = s[HALO:] / jnp.minimum(t1, w).astype(F32)
            mixed = (mean - u).astype(BF16)
            y = jnp.dot(mixed, pw_ref[...].astype(BF16), preferred_element_type=F32)
            o_ref[...] = (y * ps_ref[...]).astype(o_ref.dtype)


def pool_mix(proj, col, pool_w, pool_scale, ts=2048):
    bsz, s, _ = proj.shape
    n_g, gd, _ = pool_w.shape
    assert n_g == len(POOL_WINDOWS) and col % gd == 0
    ts = _tile(s, ts, HALO)
    cb, hb = col // gd, ts // HALO
    return pl.pallas_call(
        functools.partial(_pool_body, ts=ts),
        grid=(bsz, s // ts, n_g),
        in_specs=[pl.BlockSpec((None, ts, gd), lambda b, i, g: (b, i, cb + g)),
                  pl.BlockSpec((None, HALO, gd), lambda b, i, g: (b, jnp.maximum(i * hb - 1, 0), cb + g)),
                  pl.BlockSpec((None, gd, gd), lambda b, i, g: (g, 0, 0)),
                  pl.BlockSpec((1, gd), lambda b, i, g: (0, g))],
        out_specs=pl.BlockSpec((None, ts, gd), lambda b, i, g: (b, i, g)),
        out_shape=jax.ShapeDtypeStruct((bsz, s, n_g * gd), BF16),
        compiler_params=_params(("parallel", "parallel", "parallel")),
        name="pool_mix",
    )(proj, proj, pool_w, pool_scale.reshape(1, n_g * gd))


CHUNK_GROUP = 2


def _for_chunks(n, fn):
    def group(jj, carry):
        for u in range(CHUNK_GROUP):
            fn(CHUNK_GROUP * jj + u)
        return carry

    def single(j, carry):
        fn(j)
        return carry

    lax.fori_loop(0, n // CHUNK_GROUP, group, 0)
    lax.fori_loop(n - n % CHUNK_GROUP, n, single, 0)


def _attn_body(lam_ref, g_ref, q_ref, k_ref, v_ref, o_ref, s_ref, m_ref, l_ref, acc_ref, *, tq, lambda_init):
    dh = DA_HEAD_DIM
    i = pl.program_id(2)
    maps = range(2)
    slabs = [slice(b * LANES, (b + 1) * LANES) for b in range(tq // LANES)]
    q = (q_ref[...].astype(F32) * (dh ** -0.5 * LOG2E)).astype(BF16)

    def rows(j):
        return pl.ds(pl.multiple_of(j * tq, tq), tq)

    def scores(j, c):
        cols = slice(c * dh, (c + 1) * dh)
        return lax.dot_general(q[:, cols], k_ref[rows(j), cols], (((1,), (1,)), ((), ())),
                               preferred_element_type=F32)

    def keep(j, c, s):
        s_ref[c, j] = s
        m = m_ref[c]
        for sl in slabs:
            m = jnp.maximum(m, s[:, sl])
        m_ref[c] = m

    def pass1(j):
        for c in maps:
            keep(j, c, scores(j, c))

    m_ref[...] = jnp.full_like(m_ref, -jnp.inf)
    _for_chunks(i, pass1)
    row = lax.broadcasted_iota(jnp.int32, (tq, tq), 0)
    col = lax.broadcasted_iota(jnp.int32, (tq, tq), 1)
    for c in maps:
        keep(i, c, jnp.where(col <= row, scores(i, c), -jnp.inf))
    mb = [jnp.broadcast_to(jnp.max(m_ref[c], axis=-1, keepdims=True), (tq, LANES)) for c in maps]

    def pass2(j):
        p_maps = []
        for c in maps:
            s = s_ref[c, j]
            l = l_ref[c]
            ps = []
            for sl in slabs:
                p = jnp.exp2(s[:, sl] - mb[c])
                l = l + p
                ps.append(p.astype(BF16))
            l_ref[c] = l
            p_maps.append(jnp.concatenate(ps, axis=1))
        acc_ref[...] += jnp.dot(jnp.concatenate(p_maps, axis=0), v_ref[rows(j), :], preferred_element_type=F32)

    l_ref[...] = jnp.zeros_like(l_ref)
    acc_ref[...] = jnp.zeros_like(acc_ref)
    _for_chunks(i + 1, pass2)
    outs = [acc_ref[c * tq:(c + 1) * tq] / jnp.sum(l_ref[c], axis=-1, keepdims=True) for c in maps]

    lp = lam_ref[...]
    lam = (jnp.exp(jnp.sum(lp[0:1] * lp[1:2], axis=-1, keepdims=True))
           - jnp.exp(jnp.sum(lp[2:3] * lp[3:4], axis=-1, keepdims=True)) + lambda_init)
    o = outs[0] - lam * outs[1]
    o = o * lax.rsqrt(jnp.mean(o * o, axis=-1, keepdims=True) + SUBLN_EPS)
    o_ref[...] = (o * g_ref[...] * (1.0 - lambda_init)).astype(o_ref.dtype)


def diff_attention(proj, col, lam, subln_g, lambda_init, tq=512):
    bsz, s, _ = proj.shape
    hw = 2 * DA_HEAD_DIM
    assert col % hw == 0
    tq = _tile(s, tq)
    qb = col // hw
    kb, vb = qb + DA_HEADS, qb + 2 * DA_HEADS
    return pl.pallas_call(
        functools.partial(_attn_body, tq=tq, lambda_init=lambda_init),
        grid=(bsz, DA_HEADS, s // tq),
        in_specs=[pl.BlockSpec((4, DA_HEAD_DIM), lambda b, h, i: (0, 0)),
                  pl.BlockSpec((1, hw), lambda b, h, i: (0, 0)),
                  pl.BlockSpec((None, tq, hw), lambda b, h, i: (b, i, qb + h)),
                  pl.BlockSpec((None, s, hw), lambda b, h, i: (b, 0, kb + h)),
                  pl.BlockSpec((None, s, hw), lambda b, h, i: (b, 0, vb + h))],
        out_specs=pl.BlockSpec((None, tq, hw), lambda b, h, i: (b, i, h)),
        out_shape=jax.ShapeDtypeStruct((bsz, s, DA_HEADS * hw), BF16),
        scratch_shapes=[pltpu.VMEM((2, s // tq, tq, tq), F32), pltpu.VMEM((2, tq, LANES), F32),
                        pltpu.VMEM((2, tq, LANES), F32), pltpu.VMEM((2 * tq, hw), F32)],
        compiler_params=_params(("parallel", "parallel", "arbitrary")),
        name="diff_attention",
    )(lam, subln_g.reshape(1, hw), proj, proj, proj)


MOE_ROWS = 512
MOE_TOKENS = 256


def _route(ei, n_e):
    n = ei.shape[0]
    e = ei[:, :TOP_K]
    onehot = (e[:, :, None] == jnp.arange(n_e, dtype=jnp.int32)).astype(jnp.int32).reshape(n * TOP_K, n_e)
    incl = jnp.cumsum(onehot, axis=0)
    rank = jnp.sum((incl - onehot) * onehot, axis=1)
    tiles = (incl[-1] + MOE_ROWS - 1) // MOE_ROWS
    tile_end = jnp.cumsum(tiles)
    start = (tile_end - tiles) * MOE_ROWS
    dest = (jnp.sum(onehot * start[None, :], axis=1) + rank).reshape(n, TOP_K)
    max_tiles = (n * TOP_K) // MOE_ROWS + n_e
    n_used = tile_end[-1:]
    t = jnp.minimum(jnp.arange(max_tiles, dtype=jnp.int32), n_used[0] - 1)
    tile_expert = jnp.sum((t[:, None] >= tile_end[None, :]).astype(jnp.int32), axis=1)
    return dest.astype(jnp.int32), tile_expert.astype(jnp.int32), n_used.astype(jnp.int32), max_tiles


def _dispatch_body(dest_ref, hp_ref, init_ref, hs_ref, sem, *, tt):
    del init_ref

    def row_copy(t, k):
        return pltpu.make_async_copy(hp_ref.at[pl.ds(t, 1)], hs_ref.at[pl.ds(dest_ref[0, TOP_K * t + k], 1)], sem)

    def start(t, carry):
        for k in range(TOP_K):
            row_copy(t, k).start()
        return carry

    lax.fori_loop(0, tt, start, 0, unroll=4)
    all_rows = hs_ref.at[pl.ds(0, TOP_K * tt)]
    pltpu.make_async_copy(all_rows, all_rows, sem).wait()


def moe_dispatch(hp, dest, n_rows):
    n, w = hp.shape
    tt = _tile(n, MOE_TOKENS, 8)
    return pl.pallas_call(
        functools.partial(_dispatch_body, tt=tt),
        grid=(n // tt,),
        in_specs=[pl.BlockSpec((None, 1, TOP_K * tt), lambda i: (i, 0, 0), memory_space=pltpu.SMEM),
                  pl.BlockSpec((tt, w), lambda i: (i, 0)),
                  pl.BlockSpec(memory_space=pl.ANY)],
        out_specs=pl.BlockSpec(memory_space=pl.ANY),
        out_shape=jax.ShapeDtypeStruct((n_rows, w), U32),
        scratch_shapes=[pltpu.SemaphoreType.DMA(())],
        input_output_aliases={2: 0},
        compiler_params=_params(("arbitrary",)),
        name="moe_dispatch",
    )(dest.reshape(n // tt, 1, TOP_K * tt), hp, jnp.zeros((n_rows, w), U32))


def _zero_unused_tile(nu_ref, o_ref):
    @pl.when(pl.program_id(1) >= nu_ref[0])
    def _():
        o_ref[...] = jnp.zeros_like(o_ref)


def _moe_glu_body(te_ref, nu_ref, a_ref, wg_ref, wu_ref, o_ref):
    del te_ref
    _zero_unused_tile(nu_ref, o_ref)

    @pl.when(pl.program_id(1) < nu_ref[0])
    def _():
        lo, hi = _unpack_halves(a_ref[...])
        half = lo.shape[1]

        def proj(w_ref):
            return (jnp.dot(lo, w_ref[:half].astype(BF16), preferred_element_type=F32)
                    + jnp.dot(hi, w_ref[half:].astype(BF16), preferred_element_type=F32))

        o_ref[...] = _silu_mul(proj(wg_ref), proj(wu_ref)).astype(o_ref.dtype)


def moe_glu(hs, wg, wu, layer, tile_expert, n_used, tn=512):
    rows, half = hs.shape
    f = wg.shape[3]
    tn = _tile(f, tn, MXU_EDGE)
    n_tiles = rows // MOE_ROWS

    def row_tile(r, nu):
        return jnp.minimum(r, nu[0] - 1)

    w_spec = pl.BlockSpec((None, None, 2 * half, tn), lambda j, r, te, nu: (layer, te[r], 0, j))
    return pl.pallas_call(
        _moe_glu_body,
        grid_spec=pltpu.PrefetchScalarGridSpec(
            num_scalar_prefetch=2,
            grid=(f // tn, n_tiles),
            in_specs=[pl.BlockSpec((MOE_ROWS, half), lambda j, r, te, nu: (row_tile(r, nu), 0)), w_spec, w_spec],
            out_specs=pl.BlockSpec((MOE_ROWS, tn), lambda j, r, te, nu: (r, j))),
        out_shape=jax.ShapeDtypeStruct((rows, f), BF16),
        compiler_params=_params(("arbitrary", "arbitrary")),
        name="moe_glu",
    )(tile_expert, n_used, hs, wg, wu)


def _moe_down_body(te_ref, nu_ref, a_ref, w_ref, o_ref):
    del te_ref
    _zero_unused_tile(nu_ref, o_ref)

    @pl.when(pl.program_id(1) < nu_ref[0])
    def _():
        o_ref[...] = jnp.dot(a_ref[...], w_ref[...].astype(BF16), preferred_element_type=F32)


def moe_down(act, wd, layer, tile_expert, n_used, tn=1024):
    rows, f = act.shape
    d = wd.shape[3]
    tn = _tile(d, tn)
    n_tiles = rows // MOE_ROWS

    def row_tile(r, nu):
        return jnp.minimum(r, nu[0] - 1)

    return pl.pallas_call(
        _moe_down_body,
        grid_spec=pltpu.PrefetchScalarGridSpec(
            num_scalar_prefetch=2,
            grid=(d // tn, n_tiles),
            in_specs=[pl.BlockSpec((MOE_ROWS, f), lambda j, r, te, nu: (row_tile(r, nu), 0)),
                      pl.BlockSpec((None, None, f, tn), lambda j, r, te, nu: (layer, te[r], 0, j))],
            out_specs=pl.BlockSpec((MOE_ROWS, tn), lambda j, r, te, nu: (r, j))),
        out_shape=jax.ShapeDtypeStruct((rows, d), F32),
        compiler_params=_params(("arbitrary", "arbitrary")),
        name="moe_down",
    )(tile_expert, n_used, act, wd)


def _combine_body(dest_ref, next_ref, ew_ref, x_ref, *rest, tt, has_norm):
    if has_norm:
        g_ref, ys_ref, o_ref, ybuf, sem = rest
    else:
        ys_ref, o_ref, ybuf, sem = rest
    i, n_steps = pl.program_id(0), pl.num_programs(0)
    slot = i % 2

    def row_copy(d_ref, sl, t, k):
        return pltpu.make_async_copy(ys_ref.at[pl.ds(d_ref[0, TOP_K * t + k], 1)],
                                     ybuf.at[sl, k, pl.ds(t, 1)], sem.at[sl])

    def start_all(d_ref, sl):
        def start(t, carry):
            for k in range(TOP_K):
                row_copy(d_ref, sl, t, k).start()
            return carry

        lax.fori_loop(0, tt, start, 0, unroll=4)

    @pl.when(i == 0)
    def _():
        start_all(dest_ref, slot)

    @pl.when(i + 1 < n_steps)
    def _():
        start_all(next_ref, 1 - slot)

    pltpu.make_async_copy(ybuf.at[slot], ybuf.at[slot], sem.at[slot]).wait()
    ew = ew_ref[...]
    y = x_ref[...] + ew[:, 0:1] * ybuf[slot, 0] + ew[:, 1:2] * ybuf[slot, 1]
    o_ref[...] = _rms(y, g_ref[...]) if has_norm else y


def moe_combine(x, ys, dest, ew, norm_g=None):
    n, d = x.shape
    tt = _tile(n, MOE_TOKENS, 8)
    n_steps = n // tt
    dest = dest.reshape(n_steps, 1, TOP_K * tt)

    def d_spec(ahead):
        return pl.BlockSpec((None, 1, TOP_K * tt), lambda i: (jnp.minimum(i + ahead, n_steps - 1), 0, 0),
                            memory_space=pltpu.SMEM)

    in_specs = [d_spec(0), d_spec(1),
                pl.BlockSpec((tt, LANES), lambda i: (i, 0)),
                pl.BlockSpec((tt, d), lambda i: (i, 0))]
    args = [dest, dest, ew, x]
    if norm_g is not None:
        in_specs.append(pl.BlockSpec((1, d), lambda i: (0, 0)))
        args.append(norm_g.reshape(1, d))
    return pl.pallas_call(
        functools.partial(_combine_body, tt=tt, has_norm=norm_g is not None),
        grid=(n_steps,),
        in_specs=in_specs + [pl.BlockSpec(memory_space=pl.ANY)],
        out_specs=pl.BlockSpec((tt, d), lambda i: (i, 0)),
        out_shape=jax.ShapeDtypeStruct((n, d), F32),
        scratch_shapes=[pltpu.VMEM((2, TOP_K, tt, d), F32), pltpu.SemaphoreType.DMA((2,))],
        compiler_params=_params(("arbitrary",)),
        name="moe_combine",
    )(*args, ys)


def moe_ffn(x, g, w_router, wg, wu, wd, layer, norm_g=None):
    n_e = wg.shape[1]
    hp, ei, ew = rmsnorm_router(x, g, w_router)
    dest, tile_expert, n_used, max_tiles = _route(ei, n_e)
    hs = moe_dispatch(hp, dest, max_tiles * MOE_ROWS)
    act = moe_glu(hs, wg, wu, layer, tile_expert, n_used)
    ys = moe_down(act, wd, layer, tile_expert, n_used)
    return moe_combine(x, ys, dest, ew, norm_g)


def _mixer(x, h, l, w_in, conv_w, w_a_out, pool_w, pool_scale, w_p_out, lam, subln, w_c_out, w_o,
           lambda_init, bsz, s):
    n, d = x.shape
    cw, pw, aw = conv_w.shape[2], pool_scale.shape[1], w_c_out.shape[1]
    proj = matmul(h, w_in, l, BF16)
    proj3 = proj.reshape(bsz, s, proj.shape[1])
    ya = short_conv(proj3, conv_w[l]).reshape(n, cw)
    yp = pool_mix(proj3, 3 * cw, pool_w[l], pool_scale[l]).reshape(n, pw)
    yc = diff_attention(proj3, 3 * cw + pw, lam[l], subln[l], lambda_init).reshape(n, aw)
    merged = merge(ya, yp, yc, w_a_out, w_p_out, w_c_out, l, proj, 3 * cw + pw + 3 * aw)
    return matmul(merged, w_o, l, F32, res=x, tn=256)


def kernel(x, norm_mix, w_in, conv_w, w_a_out, pool_w, pool_scale, w_p_out, lam, subln, w_c_out, w_o,
           norm_ffn, ffn_w_gate, ffn_w_up, ffn_w_down, w_router, moe_w_gate, moe_w_up, moe_w_down,
           norm_final):
    bsz, s, d = x.shape
    depth = norm_mix.shape[0]
    n = bsz * s
    x = x.reshape(n, d)
    for l in range(depth):
        lambda_init = 0.8 - 0.6 * math.exp(-0.3 * l)
        h = rmsnorm(x, norm_mix[l], BF16)
        x = _mixer(x, h, l, w_in, conv_w, w_a_out, pool_w, pool_scale, w_p_out, lam, subln, w_c_out, w_o,
                   lambda_init, bsz, s)
        j = l // 2
        if l % 2 == 0:
            h = rmsnorm(x, norm_ffn[l], BF16)
            act = glu_in(h, ffn_w_gate, ffn_w_up, j)
            x = matmul_acc(act, ffn_w_down, j, x)
        elif l < depth - 1:
            x = moe_ffn(x, norm_ffn[l], w_router[j], moe_w_gate, moe_w_up, moe_w_down, j)
        else:
            return moe_ffn(x, norm_ffn[l], w_router[j], moe_w_gate, moe_w_up, moe_w_down, j,
                           norm_g=norm_final).reshape(bsz, s, d)
    return rmsnorm(x, norm_final, F32).reshape(bsz, s, d)
```
